```python
import math
import jax, jax.numpy as jnp
from jax import lax
import numpy as np

D_MODEL = 1024
BATCH = 8
SEQ = 2048
DEPTH = 4
DEC_BATCH = 128
DEC_SEQ = 4
PAST_LEN = 2048
PAGE_SIZE = 128

N_MIXERS = 3
N_A = (DEPTH + 2) // 3
N_B = (DEPTH + 1) // 3
N_C = DEPTH // 3
N_DENSE = (DEPTH + 1) // 2
N_MOE = DEPTH // 2

RWKV_HEAD = 64
RWKV_HEADS = D_MODEL // RWKV_HEAD
LORA_DECAY = 64
LORA_AAA = 64
LORA_MV = 32
LORA_GATE = 128
GN_EPS = 64e-5
CONV_WIDTH = 31
CONV_BUF = CONV_WIDTH - 1
LN_EPS = 1e-5
ATT_HEADS = 8
ATT_DH = D_MODEL // ATT_HEADS // 2
ATT_VD = 2 * ATT_DH
ATT_SCALE = ATT_DH ** -0.5
Q_BLOCK = 128
SUBLN_EPS = 1e-5
D_FF = 2816
N_EXPERTS = 8
TOP_K = 2
D_FF_EXPERT = 3584
RMS_EPS = 1e-6

kernel_name = 'hybrid_rwkv7_conformer_diffattn_step'


def _rms_norm(x, g, eps=RMS_EPS):
    xf = x.astype(jnp.float32)
    y = xf * lax.rsqrt(jnp.mean(xf * xf, axis=-1, keepdims=True) + eps)
    return (y * g.astype(jnp.float32)).astype(x.dtype)


def _layer_norm(x, g, b, eps):
    xf = x.astype(jnp.float32)
    mu = jnp.mean(xf, axis=-1, keepdims=True)
    xc = xf - mu
    var = jnp.mean(xc * xc, axis=-1, keepdims=True)
    return (xc * lax.rsqrt(var + eps) * g.astype(jnp.float32) + b.astype(jnp.float32)).astype(x.dtype)


def _wkv7_step(S, inp):
    r, w, k, v, a, b = inp
    sa = jnp.einsum('bhvk,bhk->bhv', S, a)
    S = S * w[:, :, None, :] + sa[..., None] * b[:, :, None, :] + v[..., None] * k[:, :, None, :]
    return S, jnp.einsum('bhvk,bhk->bhv', S, r)


def _rwkv7_mix(h, shift_prev, S0, v_first, vres, mu, w_rkv, w0, w1, w2, a0, a1, a2,
               g1, g2, k_k, k_a, r_k, ln_g, ln_b, w_o):
    B, T, C = h.shape
    H, N = RWKV_HEADS, RWKV_HEAD
    h_prev = jnp.concatenate([shift_prev[:, None, :].astype(h.dtype), h[:, :-1]], axis=1)
    xx = h_prev - h
    xr, xw, xk, xv, xa, xg = [h + xx * mu[n] for n in range(6)]
    r, k, v = jnp.einsum('nbtc,ncd->nbtd', jnp.stack([xr, xk, xv]), w_rkv)
    w_log = -jax.nn.softplus(-(w0 + jnp.tanh(xw @ w1) @ w2)) - 0.5
    decay = jnp.exp(-jnp.exp(w_log.astype(jnp.float32)))
    if vres is None:
        v_first = v
    else:
        v0, v1, v2 = vres
        v = v + (v_first - v) * jax.nn.sigmoid(v0 + (xv @ v1) @ v2)
    a = jax.nn.sigmoid(a0 + (xa @ a1) @ a2)
    g = jax.nn.sigmoid(xg @ g1) @ g2
    kk = (k * k_k).astype(jnp.float32).reshape(B, T, H, N)
    kk = kk / jnp.maximum(jnp.sqrt(jnp.sum(kk * kk, axis=-1, keepdims=True)), 1e-12)
    k = k * (1 + (a - 1) * k_a)

    def heads(z):
        return jnp.transpose(z.astype(jnp.float32).reshape(B, T, H, N), (1, 0, 2, 3))

    kk_t = jnp.transpose(kk, (1, 0, 2, 3))
    seq_in = (heads(r), heads(decay), heads(k), heads(v), -kk_t, kk_t * heads(a))
    S_fin, y = lax.scan(_wkv7_step, S0.astype(jnp.float32), seq_in)
    y = jnp.transpose(y, (1, 0, 2, 3))
    ym = jnp.mean(y, axis=-1, keepdims=True)
    yc = y - ym
    y = yc * lax.rsqrt(jnp.mean(yc * yc, axis=-1, keepdims=True) + GN_EPS)
    y = y.reshape(B, T, C) * ln_g.astype(jnp.float32) + ln_b.astype(jnp.float32)
    rh = r.astype(jnp.float32).reshape(B, T, H, N)
    kh = k.astype(jnp.float32).reshape(B, T, H, N)
    bonus = jnp.sum(rh * kh * r_k.astype(jnp.float32), axis=-1, keepdims=True) * v.astype(jnp.float32).reshape(B, T, H, N)
    y = (y + bonus.reshape(B, T, C)).astype(h.dtype)
    out = (y * g) @ w_o
    return out, h[:, -1], S_fin.astype(S0.dtype), v_first


def _conv_module(h, buf, w_pw1, b_pw1, w_dw, b_dw, ln_g, ln_b, w_pw2, b_pw2):
    u = h @ w_pw1 + b_pw1
    u = u[..., :D_MODEL] * jax.nn.sigmoid(u[..., D_MODEL:])
    ext = jnp.concatenate([buf.astype(u.dtype), u], axis=1)
    c = lax.conv_general_dilated(ext, w_dw[:, None, :].astype(ext.dtype), (1,), 'VALID',
                                 dimension_numbers=('NWC', 'WIO', 'NWC'),
                                 feature_group_count=D_MODEL) + b_dw
    z = jax.nn.silu(_layer_norm(c, ln_g, ln_b, LN_EPS))
    return z @ w_pw2 + b_pw2, ext[:, -CONV_BUF:]


def _diff_qkv(h, w_qkv):
    B, T, _ = h.shape
    qkv = h @ w_qkv
    q = qkv[..., :D_MODEL].reshape(B, T, ATT_HEADS, 2, ATT_DH)
    k = qkv[..., D_MODEL:2 * D_MODEL].reshape(B, T, ATT_HEADS, 2, ATT_DH)
    v = qkv[..., 2 * D_MODEL:].reshape(B, T, ATT_HEADS, ATT_VD)
    return q, k, v


def _diff_lambda(lam_p, lam_init):
    lp = lam_p.astype(jnp.float32)
    return jnp.exp(jnp.sum(lp[0] * lp[1])) - jnp.exp(jnp.sum(lp[2] * lp[3])) + lam_init


def _diff_attn_prompt(q, k, v, lam):
    B, T = q.shape[:2]
    nblk = T // Q_BLOCK
    qb = jnp.moveaxis(q.reshape(B, nblk, Q_BLOCK, ATT_HEADS, 2, ATT_DH), 1, 0)
    kpos = jnp.arange(T)
    vf = v.astype(jnp.float32)

    def block(args):
        qi, start = args
        s = jnp.einsum('bqhcd,bkhcd->bhcqk', qi, k, preferred_element_type=jnp.float32) * ATT_SCALE
        qpos = start + jnp.arange(Q_BLOCK)
        s = jnp.where(kpos[None, :] <= qpos[:, None], s, -jnp.inf)
        p = jax.nn.softmax(s, axis=-1)
        att = p[:, :, 0] - lam * p[:, :, 1]
        return jnp.einsum('bhqk,bkhe->bqhe', att, vf)

    o = lax.map(block, (qb, jnp.arange(nblk) * Q_BLOCK))
    return jnp.moveaxis(o, 0, 1).reshape(B, T, ATT_HEADS, ATT_VD)


def _diff_attn_sample(q, k_new, v_new, cache_k, cache_v, page_table, layer, lam):
    Bd, S = q.shape[:2]
    f32 = jnp.float32
    qf = q.astype(f32) * ATT_SCALE

    def merge(carry, s, vb):
        m, l, acc = carry
        m_new = jnp.maximum(m, jnp.max(s, axis=-1))
        alpha = jnp.exp(m - m_new)
        p = jnp.exp(s - m_new[..., None])
        l = l * alpha + jnp.sum(p, axis=-1)
        acc = acc * alpha[..., None] + jnp.einsum('bhcqk,bkhe->bhcqe', p, vb.astype(f32))
        return (m_new, l, acc)

    def page_step(carry, pages):
        kp = cache_k[layer, pages].reshape(Bd, PAGE_SIZE, ATT_HEADS, 2, ATT_DH).astype(f32)
        vp = cache_v[layer, pages]
        s = jnp.einsum('bqhcd,bkhcd->bhcqk', qf, kp)
        return merge(carry, s, vp), None

    init = (jnp.full((Bd, ATT_HEADS, 2, S), -jnp.inf, f32),
            jnp.zeros((Bd, ATT_HEADS, 2, S), f32),
            jnp.zeros((Bd, ATT_HEADS, 2, S, ATT_VD), f32))
    carry, _ = lax.scan(page_step, init, page_table.T)
    s = jnp.einsum('bqhcd,bkhcd->bhcqk', qf, k_new.astype(f32))
    s = jnp.where(jnp.tril(jnp.ones((S, S), bool)), s, -jnp.inf)
    m, l, acc = merge(carry, s, v_new)
    o = acc / l[..., None]
    o = o[:, :, 0] - lam * o[:, :, 1]
    return jnp.transpose(o, (0, 2, 1, 3))


def _diff_out(o, subln, lam_init, w_o):
    B, T = o.shape[:2]
    o = _rms_norm(o, subln, SUBLN_EPS) * (1.0 - lam_init)
    return o.reshape(B, T, D_MODEL).astype(w_o.dtype) @ w_o


def _swiglu(x, w13, w2):
    gate, up = jnp.split(x @ w13, 2, axis=-1)
    return (jax.nn.silu(gate) * up) @ w2


def _moe(x, w_router, w13, w2):
    logits = (x @ w_router).astype(jnp.float32)
    top_v, top_i = lax.top_k(logits, TOP_K)
    gates = jax.nn.softmax(top_v, axis=-1)
    dense_gate = jnp.sum(jax.nn.one_hot(top_i, N_EXPERTS, dtype=jnp.float32) * gates[..., None], axis=-2)
    y = jnp.zeros_like(x)
    for e in range(N_EXPERTS):
        y = y + dense_gate[..., e:e + 1].astype(x.dtype) * _swiglu(x, w13[e], w2[e])
    return y


def setup_inputs(seed: int = 0) -> dict:
    key = jax.random.key(seed)
    ks = iter(jax.random.split(key, 64))
    f32 = jnp.float32

    def nrm(shape, scale):
        return jax.random.normal(next(ks), shape, f32) * scale

    def gain(shape):
        return 1.0 + nrm(shape, 0.05)

    D = D_MODEL
    H, N = RWKV_HEADS, RWKV_HEAD
    n_pages = PAST_LEN // PAGE_SIZE
    n_pool = (DEC_BATCH * n_pages * 5 + 3) // 4
    perm = jax.random.permutation(next(ks), n_pool)
    page_table = perm[:DEC_BATCH * n_pages].reshape(DEC_BATCH, n_pages).astype(jnp.int32)
    inputs = {}
    inputs['x_prompt'] = nrm((BATCH, SEQ, D), 1.0)
    inputs['x_sample'] = nrm((DEC_BATCH, DEC_SEQ, D), 1.0)
    inputs['state_wkv'] = nrm((N_A, DEC_BATCH, H, N, N), 0.5)
    inputs['state_shift'] = nrm((N_A, DEC_BATCH, D), 1.0)
    inputs['state_conv'] = nrm((N_B, DEC_BATCH, CONV_BUF, D), 0.5)
    inputs['cache_k'] = nrm((N_C, n_pool, PAGE_SIZE, ATT_HEADS, 2 * ATT_DH), 1.0)
    inputs['cache_v'] = nrm((N_C, n_pool, PAGE_SIZE, ATT_HEADS, ATT_VD), 1.0)
    inputs['page_table'] = page_table
    inputs['norm_mix'] = gain((DEPTH, D))
    inputs['norm_ffn'] = gain((DEPTH, D))
    inputs['norm_final'] = gain((D,))
    inputs['rwkv_mu'] = jax.random.uniform(next(ks), (N_A, 6, D), f32)
    inputs['rwkv_w_rkv'] = nrm((N_A, 3, D, D), D ** -0.5)
    inputs['rwkv_w0'] = -2.0 + nrm((N_A, D), 1.0)
    inputs['rwkv_w1'] = nrm((N_A, D, LORA_DECAY), D ** -0.5)
    inputs['rwkv_w2'] = nrm((N_A, LORA_DECAY, D), 0.1 * LORA_DECAY ** -0.5)
    inputs['rwkv_a0'] = nrm((N_A, D), 0.1)
    inputs['rwkv_a1'] = nrm((N_A, D, LORA_AAA), D ** -0.5)
    inputs['rwkv_a2'] = nrm((N_A, LORA_AAA, D), 0.1 * LORA_AAA ** -0.5)
    inputs['rwkv_v0'] = nrm((N_A - 1, D), 0.1)
    inputs['rwkv_v1'] = nrm((N_A - 1, D, LORA_MV), D ** -0.5)
    inputs['rwkv_v2'] = nrm((N_A - 1, LORA_MV, D), 0.1 * LORA_MV ** -0.5)
    inputs['rwkv_g1'] = nrm((N_A, D, LORA_GATE), D ** -0.5)
    inputs['rwkv_g2'] = nrm((N_A, LORA_GATE, D), LORA_GATE ** -0.5)
    inputs['rwkv_k_k'] = 1.0 + nrm((N_A, D), 0.1)
    inputs['rwkv_k_a'] = 1.0 + nrm((N_A, D), 0.1)
    inputs['rwkv_r_k'] = nrm((N_A, H, N), 0.1)
    inputs['rwkv_ln_g'] = gain((N_A, D))
    inputs['rwkv_ln_b'] = nrm((N_A, D), 0.02)
    inputs['rwkv_w_o'] = nrm((N_A, D, D), D ** -0.5)
    inputs['conv_w_pw1'] = nrm((N_B, D, 2 * D), D ** -0.5)
    inputs['conv_b_pw1'] = nrm((N_B, 2 * D), 0.02)
    inputs['conv_w_dw'] = nrm((N_B, CONV_WIDTH, D), CONV_WIDTH ** -0.5)
    inputs['conv_b_dw'] = nrm((N_B, D), 0.02)
    inputs['conv_ln_g'] = gain((N_B, D))
    inputs['conv_ln_b'] = nrm((N_B, D), 0.02)
    inputs['conv_w_pw2'] = nrm((N_B, D, D), D ** -0.5)
    inputs['conv_b_pw2'] = nrm((N_B, D), 0.02)
    inputs['attn_w_qkv'] = nrm((N_C, D, 3 * D), D ** -0.5)
    inputs['attn_lambda'] = nrm((N_C, 4, ATT_DH), 0.1)
    inputs['attn_subln'] = gain((N_C, ATT_VD))
    inputs['attn_w_o'] = nrm((N_C, D, D), D ** -0.5)
    inputs['ffn_w13'] = nrm((N_DENSE, D, 2 * D_FF), D ** -0.5)
    inputs['ffn_w2'] = nrm((N_DENSE, D_FF, D), D_FF ** -0.5)
    inputs['moe_router'] = nrm((N_MOE, D, N_EXPERTS), D ** -0.5)
    inputs['moe_w13'] = nrm((N_MOE, N_EXPERTS, D, 2 * D_FF_EXPERT), D ** -0.5)
    inputs['moe_w2'] = nrm((N_MOE, N_EXPERTS, D_FF_EXPERT, D), D_FF_EXPERT ** -0.5)
    return inputs


def reference(x_prompt, x_sample, state_wkv, state_shift, state_conv, cache_k, cache_v, page_table,
              norm_mix, norm_ffn, norm_final,
              rwkv_mu, rwkv_w_rkv, rwkv_w0, rwkv_w1, rwkv_w2, rwkv_a0, rwkv_a1, rwkv_a2,
              rwkv_v0, rwkv_v1, rwkv_v2, rwkv_g1, rwkv_g2, rwkv_k_k, rwkv_k_a, rwkv_r_k,
              rwkv_ln_g, rwkv_ln_b, rwkv_w_o,
              conv_w_pw1, conv_b_pw1, conv_w_dw, conv_b_dw, conv_ln_g, conv_ln_b, conv_w_pw2, conv_b_pw2,
              attn_w_qkv, attn_lambda, attn_subln, attn_w_o,
              ffn_w13, ffn_w2, moe_router, moe_w13, moe_w2):

    def run(x, wkv_in, shift_in, conv_in, attend):
        wkv_out, shift_out, conv_out, k_out, v_out = [], [], [], [], []
        v_first = None
        for i in range(DEPTH):
            h = _rms_norm(x, norm_mix[i])
            kind = i % N_MIXERS
            j = i // N_MIXERS
            if kind == 0:
                vres = None if j == 0 else (rwkv_v0[j - 1], rwkv_v1[j - 1], rwkv_v2[j - 1])
                mixed, sh, S, v_first = _rwkv7_mix(
                    h, shift_in[j], wkv_in[j], v_first, vres, rwkv_mu[j], rwkv_w_rkv[j],
                    rwkv_w0[j], rwkv_w1[j], rwkv_w2[j], rwkv_a0[j], rwkv_a1[j], rwkv_a2[j],
                    rwkv_g1[j], rwkv_g2[j], rwkv_k_k[j], rwkv_k_a[j], rwkv_r_k[j],
                    rwkv_ln_g[j], rwkv_ln_b[j], rwkv_w_o[j])
                wkv_out.append(S)
                shift_out.append(sh)
            elif kind == 1:
                mixed, buf = _conv_module(h, conv_in[j], conv_w_pw1[j], conv_b_pw1[j], conv_w_dw[j],
                                          conv_b_dw[j], conv_ln_g[j], conv_ln_b[j], conv_w_pw2[j], conv_b_pw2[j])
                conv_out.append(buf)
            else:
                lam_init = 0.8 - 0.6 * math.exp(-0.3 * i)
                q, k, v = _diff_qkv(h, attn_w_qkv[j])
                lam = _diff_lambda(attn_lambda[j], lam_init)
                o = attend(q, k, v, lam, j)
                mixed = _diff_out(o, attn_subln[j], lam_init, attn_w_o[j])
                B, T = k.shape[:2]
                k_out.append(k.reshape(B, T, ATT_HEADS, 2 * ATT_DH))
                v_out.append(v)
            x = x + mixed
            h = _rms_norm(x, norm_ffn[i])
            if i % 2 == 0:
                x = x + _swiglu(h, ffn_w13[i // 2], ffn_w2[i // 2])
            else:
                x = x + _moe(h, moe_router[i // 2], moe_w13[i // 2], moe_w2[i // 2])
        return (_rms_norm(x, norm_final), jnp.stack(wkv_out), jnp.stack(shift_out),
                jnp.stack(conv_out), jnp.stack(k_out), jnp.stack(v_out))

    Bp = x_prompt.shape[0]
    wkv0 = jnp.zeros((N_A, Bp) + state_wkv.shape[2:], state_wkv.dtype)
    shift0 = jnp.zeros((N_A, Bp, D_MODEL), x_prompt.dtype)
    conv0 = jnp.zeros((N_B, Bp, CONV_BUF, D_MODEL), x_prompt.dtype)

    def attend_prompt(q, k, v, lam, j):
        return _diff_attn_prompt(q, k, v, lam)

    def attend_sample(q, k, v, lam, j):
        return _diff_attn_sample(q, k, v, cache_k, cache_v, page_table, j, lam)

    y_p, wkv_p, sh_p, conv_p, k_p, v_p = run(x_prompt, wkv0, shift0, conv0, attend_prompt)
    y_s, wkv_s, sh_s, conv_s, k_s, v_s = run(x_sample, state_wkv, state_shift, state_conv, attend_sample)
    return (y_p, y_s, wkv_p, wkv_s, sh_p, sh_s, conv_p, conv_s, k_p, k_s, v_p, v_s)
```

```python
import functools
import math

import jax
import jax.numpy as jnp
from jax import lax
from jax.experimental import pallas as pl
from jax.experimental.pallas import tpu as pltpu

F32 = jnp.float32
BF16 = jnp.bfloat16

D_MODEL = 1024
DEPTH = 4
PAGE_SIZE = 128
RWKV_HEAD = 64
RWKV_HEADS = D_MODEL // RWKV_HEAD
GN_EPS = 64e-5
CONV_WIDTH = 31
CONV_BUF = CONV_WIDTH - 1
LN_EPS = 1e-5
ATT_HEADS = 8
ATT_DH = 64
ATT_VD = 128
ATT_SCALE = ATT_DH ** -0.5
SUBLN_EPS = 1e-5
N_EXPERTS = 8
TOP_K = 2
RMS_EPS = 1e-6

V7X_LANES = 128
V7X_SUBLANES = 8
V7X_VMEM_LIMIT_BYTES = 56 * 1024 * 1024

ROW_TILE = 512
CONV_HALO = 32
NEG_INF = float("-inf")


def _cparams(*sem):
    return pltpu.CompilerParams(dimension_semantics=sem, vmem_limit_bytes=V7X_VMEM_LIMIT_BYTES)


def _rms(x, g, eps=RMS_EPS):
    return x * lax.rsqrt(jnp.mean(x * x, axis=-1, keepdims=True) + eps) * g


def _sigmoid(x):
    return 1.0 / (1.0 + jnp.exp(-x))


def _silu(x):
    return x * _sigmoid(x)


def _row_tile(m):
    return min(ROW_TILE, m)


def _rms_body(x_ref, g_ref, o_ref):
    o_ref[...] = _rms(x_ref[...], g_ref[...]).astype(o_ref.dtype)


def _rmsnorm(x, g, out_dtype=F32):
    m, d = x.shape
    tm = _row_tile(m)
    return pl.pallas_call(
        _rms_body,
        grid=(m // tm,),
        in_specs=[pl.BlockSpec((tm, d), lambda i: (i, 0)), pl.BlockSpec((1, d), lambda i: (0, 0))],
        out_specs=pl.BlockSpec((tm, d), lambda i: (i, 0)),
        out_shape=jax.ShapeDtypeStruct((m, d), out_dtype),
        compiler_params=_cparams("parallel"),
        name="rmsnorm",
    )(x, g.reshape(1, d))


def _mm_body(*refs, norm, mul, bias, act, res, glu):
    it = iter(refs)
    x_ref = next(it)
    g_ref = next(it) if norm else None
    m_ref = next(it) if mul else None
    w_ref = next(it)
    w2_ref = next(it) if glu else None
    b_ref = next(it) if bias else None
    b2_ref = next(it) if (bias and glu) else None
    r_ref = next(it) if res else None
    o_ref = next(it)
    xs_ref = next(it)

    @pl.when(pl.program_id(1) == 0)
    def _():
        x = x_ref[...].astype(F32)
        if norm:
            x = _rms(x, g_ref[...])
        if mul:
            x = x * m_ref[...]
        xs_ref[...] = x.astype(BF16)

    xs = xs_ref[...]
    y = jnp.dot(xs, w_ref[...], preferred_element_type=F32)
    if bias:
        y = y + b_ref[...]
    if glu:
        y2 = jnp.dot(xs, w2_ref[...], preferred_element_type=F32)
        if bias:
            y2 = y2 + b2_ref[...]
        y = y * _sigmoid(y2)
    if act == "tanh":
        y = jnp.tanh(y)
    elif act == "sigmoid":
        y = _sigmoid(y)
    if res:
        y = y + r_ref[...]
    o_ref[...] = y.astype(o_ref.dtype)


def _matmul(x, w, *, norm_g=None, mul=None, bias=None, act=None, res=None, glu=False,
            col_tile=None, split_out=False):
    m, k = x.shape
    n = w.shape[1]
    n_out = n // 2 if glu else n
    tm = _row_tile(m)
    tn = col_tile or n_out
    nj = n_out // tn
    in_specs = [pl.BlockSpec((tm, k), lambda i, j: (i, 0))]
    args = [x]
    if norm_g is not None:
        in_specs.append(pl.BlockSpec((1, k), lambda i, j: (0, 0)))
        args.append(norm_g.reshape(1, k))
    if mul is not None:
        in_specs.append(pl.BlockSpec((tm, k), lambda i, j: (i, 0)))
        args.append(mul)
    in_specs.append(pl.BlockSpec((k, tn), lambda i, j: (0, j)))
    args.append(w)
    if glu:
        in_specs.append(pl.BlockSpec((k, tn), lambda i, j: (0, j + nj)))
        args.append(w)
    if bias is not None:
        b2d = bias.reshape(1, n)
        in_specs.append(pl.BlockSpec((1, tn), lambda i, j: (0, j)))
        args.append(b2d)
        if glu:
            in_specs.append(pl.BlockSpec((1, tn), lambda i, j: (0, j + nj)))
            args.append(b2d)
    if res is not None:
        in_specs.append(pl.BlockSpec((tm, tn), lambda i, j: (i, j)))
        args.append(res)
    if split_out:
        out_shape = jax.ShapeDtypeStruct((nj, m, tn), F32)
        out_spec = pl.BlockSpec((None, tm, tn), lambda i, j: (j, i, 0))
    else:
        out_shape = jax.ShapeDtypeStruct((m, n_out), F32)
        out_spec = pl.BlockSpec((tm, tn), lambda i, j: (i, j))
    body = functools.partial(_mm_body, norm=norm_g is not None, mul=mul is not None,
                             bias=bias is not None, act=act, res=res is not None, glu=glu)
    return pl.pallas_call(
        body,
        grid=(m // tm, nj),
        in_specs=in_specs,
        out_specs=out_spec,
        out_shape=out_shape,
        scratch_shapes=[pltpu.VMEM((tm, k), BF16)],
        compiler_params=_cparams("parallel", "arbitrary"),
        name="matmul",
    )(*args)


def _ffn_body(*refs, moe):
    if moe:
        te_ref, tv_ref, x_ref, g_ref, wg_ref, wu_ref, wd_ref, s_ref, o_ref, hs_ref, acc_ref = refs
    else:
        x_ref, g_ref, wg_ref, wu_ref, wd_ref, o_ref, hs_ref, acc_ref = refs
    j = pl.program_id(1)
    nj = pl.num_programs(1)

    def compute():
        @pl.when(j == 0)
        def _():
            hs_ref[...] = _rms(x_ref[...], g_ref[...]).astype(BF16)
            acc_ref[...] = jnp.zeros_like(acc_ref)

        h = hs_ref[...]
        gate = jnp.dot(h, wg_ref[...], preferred_element_type=F32)
        up = jnp.dot(h, wu_ref[...], preferred_element_type=F32)
        a = (_silu(gate) * up).astype(BF16)
        acc_ref[...] += jnp.dot(a, wd_ref[...], preferred_element_type=F32)

        @pl.when(j == nj - 1)
        def _():
            if moe:
                o_ref[...] = acc_ref[...] * s_ref[...]
            else:
                o_ref[...] = x_ref[...] + acc_ref[...]

    if moe:
        valid = tv_ref[pl.program_id(0)] > 0
        pl.when(valid)(compute)

        @pl.when(jnp.logical_and(jnp.logical_not(valid), j == nj - 1))
        def _():
            o_ref[...] = jnp.zeros_like(o_ref)
    else:
        compute()


def _ffn_dense(x, g, w13, w2, ff_tile):
    m, d = x.shape
    f = w2.shape[0]
    tm = _row_tile(m)
    nf = f // ff_tile
    return pl.pallas_call(
        functools.partial(_ffn_body, moe=False),
        grid=(m // tm, nf),
        in_specs=[
            pl.BlockSpec((tm, d), lambda i, j: (i, 0)),
            pl.BlockSpec((1, d), lambda i, j: (0, 0)),
            pl.BlockSpec((d, ff_tile), lambda i, j: (0, j)),
            pl.BlockSpec((d, ff_tile), lambda i, j: (0, j + nf)),
            pl.BlockSpec((ff_tile, d), lambda i, j: (j, 0)),
        ],
        out_specs=pl.BlockSpec((tm, d), lambda i, j: (i, 0)),
        out_shape=jax.ShapeDtypeStruct((m, d), F32),
        scratch_shapes=[pltpu.VMEM((tm, d), BF16), pltpu.VMEM((tm, d), F32)],
        compiler_params=_cparams("parallel", "arbitrary"),
        name="ffn_dense",
    )(x, g.reshape(1, d), w13, w13, w2)


def _ffn_experts(xs, g, w13, w2, row_scale, tile_expert, tile_valid, tm, ff_tile):
    s, d = xs.shape
    f = w2.shape[1]
    nf = f // ff_tile
    grid_spec = pltpu.PrefetchScalarGridSpec(
        num_scalar_prefetch=2,
        grid=(s // tm, nf),
        in_specs=[
            pl.BlockSpec((tm, d), lambda i, j, te, tv: (i, 0)),
            pl.BlockSpec((1, d), lambda i, j, te, tv: (0, 0)),
            pl.BlockSpec((None, d, ff_tile), lambda i, j, te, tv: (te[i], 0, j)),
            pl.BlockSpec((None, d, ff_tile), lambda i, j, te, tv: (te[i], 0, j + nf)),
            pl.BlockSpec((None, ff_tile, d), lambda i, j, te, tv: (te[i], j, 0)),
            pl.BlockSpec((tm, 1), lambda i, j, te, tv: (i, 0)),
        ],
        out_specs=pl.BlockSpec((tm, d), lambda i, j, te, tv: (i, 0)),
        scratch_shapes=[pltpu.VMEM((tm, d), BF16), pltpu.VMEM((tm, d), F32)],
    )
    return pl.pallas_call(
        functools.partial(_ffn_body, moe=True),
        grid_spec=grid_spec,
        out_shape=jax.ShapeDtypeStruct((s, d), F32),
        compiler_params=_cparams("parallel", "arbitrary"),
        name="ffn_experts",
    )(tile_expert, tile_valid, xs, g.reshape(1, d), w13, w13, w2, row_scale)


def _router_body(x_ref, g_ref, whi_ref, wlo_ref, o_ref):
    h = _rms(x_ref[...], g_ref[...])
    hhi = h.astype(BF16)
    hlo = (h - hhi.astype(F32)).astype(BF16)
    whi = whi_ref[...]
    lg = (jnp.dot(hhi, whi, preferred_element_type=F32)
          + jnp.dot(hlo, whi, preferred_element_type=F32)
          + jnp.dot(hhi, wlo_ref[...], preferred_element_type=F32))
    lane = lax.broadcasted_iota(jnp.int32, lg.shape, 1)
    lg = jnp.where(lane < N_EXPERTS, lg, NEG_INF)
    m1 = jnp.max(lg, axis=-1, keepdims=True)
    i1 = jnp.min(jnp.where(lg == m1, lane, V7X_LANES), axis=-1, keepdims=True)
    lg2 = jnp.where(lane == i1, NEG_INF, lg)
    m2 = jnp.max(lg2, axis=-1, keepdims=True)
    i2 = jnp.min(jnp.where(lg2 == m2, lane, V7X_LANES), axis=-1, keepdims=True)
    e2 = jnp.exp(m2 - m1)
    g1 = 1.0 / (1.0 + e2)
    g2 = e2 / (1.0 + e2)
    out = jnp.where(lane == 0, i1.astype(F32),
                    jnp.where(lane == 1, i2.astype(F32),
                              jnp.where(lane == 2, g1, jnp.where(lane == 3, g2, 0.0))))
    o_ref[...] = out


def _router(x, g, w_router):
    m, d = x.shape
    tm = _row_tile(m)
    wpad = jnp.zeros((d, V7X_LANES), F32).at[:, :N_EXPERTS].set(w_router)
    whi = wpad.astype(BF16)
    wlo = (wpad - whi.astype(F32)).astype(BF16)
    return pl.pallas_call(
        _router_body,
        grid=(m // tm,),
        in_specs=[
            pl.BlockSpec((tm, d), lambda i: (i, 0)),
            pl.BlockSpec((1, d), lambda i: (0, 0)),
            pl.BlockSpec((d, V7X_LANES), lambda i: (0, 0)),
            pl.BlockSpec((d, V7X_LANES), lambda i: (0, 0)),
        ],
        out_specs=pl.BlockSpec((tm, V7X_LANES), lambda i: (i, 0)),
        out_shape=jax.ShapeDtypeStruct((m, V7X_LANES), F32),
        compiler_params=_cparams("parallel"),
        name="router",
    )(x, g.reshape(1, d), whi, wlo)


def _gather_body(src_ref, x_hbm, o_ref, sem):
    tg = o_ref.shape[0]
    base = pl.program_id(0) * tg

    def row_copy(r):
        return pltpu.make_async_copy(x_hbm.at[pl.ds(src_ref[base + r], 1), :],
                                     o_ref.at[pl.ds(r, 1), :], sem)

    def start(r, c):
        row_copy(r).start()
        return c

    def wait(r, c):
        row_copy(r).wait()
        return c

    lax.fori_loop(0, tg, start, 0)
    lax.fori_loop(0, tg, wait, 0)


def _gather_rows(x, src, tg):
    s = src.shape[0]
    d = x.shape[1]
    grid_spec = pltpu.PrefetchScalarGridSpec(
        num_scalar_prefetch=1,
        grid=(s // tg,),
        in_specs=[pl.BlockSpec(memory_space=pl.ANY)],
        out_specs=pl.BlockSpec((tg, d), lambda i, src: (i, 0)),
        scratch_shapes=[pltpu.SemaphoreType.DMA(())],
    )
    return pl.pallas_call(
        _gather_body,
        grid_spec=grid_spec,
        out_shape=jax.ShapeDtypeStruct((s, d), x.dtype),
        compiler_params=_cparams("arbitrary"),
        name="moe_gather",
    )(src, x)


def _combine_body(dest_ref, x_ref, y_hbm, o_ref, buf, sem):
    tc = o_ref.shape[0]
    base = pl.program_id(0) * tc

    def row_copy(r, k):
        return pltpu.make_async_copy(y_hbm.at[pl.ds(dest_ref[TOP_K * (base + r) + k], 1), :],
                                     buf.at[k, pl.ds(r, 1), :], sem)

    def start(r, c):
        for k in range(TOP_K):
            row_copy(r, k).start()
        return c

    def wait(r, c):
        for k in range(TOP_K):
            row_copy(r, k).wait()
        return c

    lax.fori_loop(0, tc, start, 0)
    lax.fori_loop(0, tc, wait, 0)
    o_ref[...] = x_ref[...] + buf[0] + buf[1]


def _moe_combine(x, y_sorted, dest, tc):
    m, d = x.shape
    grid_spec = pltpu.PrefetchScalarGridSpec(
        num_scalar_prefetch=1,
        grid=(m // tc,),
        in_specs=[pl.BlockSpec((tc, d), lambda i, dest: (i, 0)), pl.BlockSpec(memory_space=pl.ANY)],
        out_specs=pl.BlockSpec((tc, d), lambda i, dest: (i, 0)),
        scratch_shapes=[pltpu.VMEM((TOP_K, tc, d), F32), pltpu.SemaphoreType.DMA(())],
    )
    return pl.pallas_call(
        _combine_body,
        grid_spec=grid_spec,
        out_shape=jax.ShapeDtypeStruct((m, d), F32),
        compiler_params=_cparams("arbitrary"),
        name="moe_combine",
    )(dest, x, y_sorted)


def _moe(x, g, w_router, w13, w2, ff_tile):
    m, d = x.shape
    tm = _row_tile(m)
    r = _router(x, g, w_router)
    e_flat = r[:, :TOP_K].astype(jnp.int32).reshape(-1)
    gate_flat = r[:, TOP_K:2 * TOP_K].reshape(-1)
    n_slots = m * TOP_K
    n_tiles = n_slots // tm + N_EXPERTS
    s_pad = n_tiles * tm
    onehot = (e_flat[:, None] == jnp.arange(N_EXPERTS, dtype=jnp.int32)[None, :]).astype(jnp.int32)
    rank = jnp.sum((jnp.cumsum(onehot, axis=0) - 1) * onehot, axis=1)
    counts = jnp.sum(onehot, axis=0)
    padded = ((counts + tm - 1) // tm) * tm
    ends = jnp.cumsum(padded)
    starts = ends - padded
    dest = (starts[e_flat] + rank).astype(jnp.int32)
    token = jnp.arange(n_slots, dtype=jnp.int32) // TOP_K
    src = jnp.zeros((s_pad,), jnp.int32).at[dest].set(token)
    scale = jnp.zeros((s_pad,), F32).at[dest].set(gate_flat).reshape(s_pad, 1)
    tile_start = jnp.arange(n_tiles, dtype=jnp.int32) * tm
    tile_valid = (tile_start < ends[-1]).astype(jnp.int32)
    tile_expert = jnp.minimum(jnp.searchsorted(ends, tile_start, side="right"), N_EXPERTS - 1)
    last_expert = jnp.max(jnp.where(tile_valid > 0, tile_expert, 0))
    tile_expert = jnp.where(tile_valid > 0, tile_expert, last_expert).astype(jnp.int32)
    xs = _gather_rows(x, src, tm)
    ys = _ffn_experts(xs, g, w13, w2, scale, tile_expert, tile_valid, tm, ff_tile)
    return _moe_combine(x, ys, dest, min(256, m))


def _wkv_body(r_ref, wl_ref, k_ref, v_ref, a_ref, kk_ref, ka_ref, rk_ref, lg_ref, lb_ref, s0_ref,
              y_ref, s_ref, w_s, a_s, b_s, k_s, r_s):
    tc = pl.program_id(1)
    n = RWKV_HEAD
    tt = r_ref.shape[0]

    @pl.when(tc == 0)
    def _():
        s_ref[...] = s0_ref[...]

    kkw = kk_ref[...]
    kaw = ka_ref[...]
    rkw = rk_ref[...]
    lgw = lg_ref[...]
    lbw = lb_ref[...]

    def step(t, carry):
        r = r_ref[t]
        k = k_ref[t]
        v = v_ref[t]
        a = a_ref[t]
        w = jnp.exp(-jnp.exp(wl_ref[t]))
        kk = k * kkw
        kk = kk / jnp.maximum(jnp.sqrt(jnp.sum(kk * kk, axis=0, keepdims=True)), 1e-12)
        k2 = k * (1.0 + (a - 1.0) * kaw)
        w_s[...] = w
        a_s[...] = -kk
        b_s[...] = kk * a
        k_s[...] = k2
        r_s[...] = r

        def bc(ref, i):
            return jnp.broadcast_to(ref[pl.ds(i, 1), :], (n, V7X_LANES))

        sa = jnp.zeros((n, V7X_LANES), F32)
        for i in range(n):
            sa = sa + s_ref[i] * bc(a_s, i)
        y = jnp.zeros((n, V7X_LANES), F32)
        for i in range(n):
            s_new = s_ref[i] * bc(w_s, i) + sa * bc(b_s, i) + v * bc(k_s, i)
            s_ref[i] = s_new
            y = y + s_new * bc(r_s, i)
        ym = jnp.mean(y, axis=0, keepdims=True)
        yc = y - ym
        yn = yc * lax.rsqrt(jnp.mean(yc * yc, axis=0, keepdims=True) + GN_EPS)
        bonus = jnp.sum(r * k2 * rkw, axis=0, keepdims=True) * v
        y_ref[t] = yn * lgw + lbw + bonus
        return carry

    lax.fori_loop(0, tt, step, 0)


def _wkv(r, wl, k, v, a, k_k, k_a, r_k, ln_g, ln_b, s0, time_tile):
    t, n, l = r.shape
    tt = min(time_tile, t)
    seq = pl.BlockSpec((tt, n, V7X_LANES), lambda i, j: (j, 0, i))
    par = pl.BlockSpec((n, V7X_LANES), lambda i, j: (0, i))
    st = pl.BlockSpec((n, n, V7X_LANES), lambda i, j: (0, 0, i))
    return pl.pallas_call(
        _wkv_body,
        grid=(l // V7X_LANES, t // tt),
        in_specs=[seq, seq, seq, seq, seq, par, par, par, par, par, st],
        out_specs=[seq, st],
        out_shape=[jax.ShapeDtypeStruct((t, n, l), F32), jax.ShapeDtypeStruct((n, n, l), F32)],
        scratch_shapes=[pltpu.VMEM((n, V7X_LANES), F32)] * 5,
        compiler_params=_cparams("parallel", "arbitrary"),
        name="wkv7",
    )(r, wl, k, v, a, k_k, k_a, r_k, ln_g, ln_b, s0)


def _to_lanes(z, b, t):
    return jnp.transpose(z.reshape(b, t, RWKV_HEADS, RWKV_HEAD), (1, 3, 0, 2)).reshape(t, RWKV_HEAD, b * RWKV_HEADS)


def _head_param(p, b):
    ph = p.reshape(RWKV_HEADS, RWKV_HEAD).T
    return jnp.tile(ph, (1, b))


def _rwkv_layer(x, b, t, shift_prev, s0, v_first, vres, norm_g, mu, w_rkv, w0, w1, w2, a0, a1, a2,
                g1, g2, k_k, k_a, r_k, ln_g, ln_b, w_o):
    m, c = x.shape
    h = _rmsnorm(x, norm_g)
    h3 = h.reshape(b, t, c)
    h_prev = jnp.concatenate([shift_prev[:, None, :], h3[:, :-1]], axis=1)
    xx = (h_prev - h3).reshape(m, c)
    xr, xw, xk, xv, xa, xg = [h + xx * mu[i] for i in range(6)]
    r = _matmul(xr, w_rkv[0])
    k = _matmul(xk, w_rkv[1])
    v = _matmul(xv, w_rkv[2])
    wl = _matmul(_matmul(xw, w1, act="tanh"), w2, bias=w0)
    wl = -jax.nn.softplus(-wl) - 0.5
    if vres is None:
        v_first = v
    else:
        v0, v1, v2 = vres
        gate = _matmul(_matmul(xv, v1), v2, bias=v0, act="sigmoid")
        v = v + (v_first - v) * gate
    a = _matmul(_matmul(xa, a1), a2, bias=a0, act="sigmoid")
    g = _matmul(_matmul(xg, g1, act="sigmoid"), g2)
    tl = lambda z: _to_lanes(z, b, t)
    hp = lambda p: _head_param(p, b)
    s0_l = jnp.transpose(s0, (3, 2, 0, 1)).reshape(RWKV_HEAD, RWKV_HEAD, b * RWKV_HEADS)
    y_l, s_l = _wkv(tl(r), tl(wl), tl(k), tl(v), tl(a), hp(k_k), hp(k_a), hp(r_k), hp(ln_g), hp(ln_b),
                    s0_l, 32)
    y = jnp.transpose(y_l.reshape(t, RWKV_HEAD, b, RWKV_HEADS), (2, 0, 3, 1)).reshape(m, c)
    s_new = jnp.transpose(s_l.reshape(RWKV_HEAD, RWKV_HEAD, b, RWKV_HEADS), (2, 3, 1, 0))
    out = _matmul(y, w_o, mul=g, res=x)
    return out, h3[:, -1], s_new, v_first


def _conv_tail(c, ln_g, ln_b):
    mu = jnp.mean(c, axis=-1, keepdims=True)
    xc = c - mu
    var = jnp.mean(xc * xc, axis=-1, keepdims=True)
    return _silu(xc * lax.rsqrt(var + LN_EPS) * ln_g + ln_b)


def _conv_prompt_body(x_ref, buf_ref, g_ref, wa_ref, wb_ref, ba_ref, bb_ref, wdw_ref, bdw_ref,
                      lg_ref, lb_ref, w2_ref, b2_ref, o_ref, tail_ref, ext_ref):
    ti = pl.program_id(1)
    tm = x_ref.shape[0]

    @pl.when(ti == 0)
    def _():
        ext_ref[pl.ds(0, CONV_HALO), :] = buf_ref[...]

    x = x_ref[...]
    h = _rms(x, g_ref[...]).astype(BF16)
    ua = jnp.dot(h, wa_ref[...], preferred_element_type=F32) + ba_ref[...]
    ub = jnp.dot(h, wb_ref[...], preferred_element_type=F32) + bb_ref[...]
    ext_ref[pl.ds(CONV_HALO, tm), :] = ua * _sigmoid(ub)
    c = jnp.zeros_like(x) + bdw_ref[...]
    for j in range(CONV_WIDTH):
        c = c + ext_ref[pl.ds(CONV_HALO - CONV_BUF + j, tm), :] * wdw_ref[pl.ds(j, 1), :]
    z = _conv_tail(c, lg_ref[...], lb_ref[...]).astype(BF16)
    o_ref[...] = x + jnp.dot(z, w2_ref[...], preferred_element_type=F32) + b2_ref[...]
    tail = ext_ref[pl.ds(tm, CONV_HALO), :]
    ext_ref[pl.ds(0, CONV_HALO), :] = tail

    @pl.when(ti == pl.num_programs(1) - 1)
    def _():
        tail_ref[...] = tail


def _conv_prompt(x, buf, norm_g, w_pw1, b_pw1, w_dw, b_dw, ln_g, ln_b, w_pw2, b_pw2, tm):
    b, t, d = x.shape
    bufp = jnp.pad(buf, ((0, 0), (CONV_HALO - CONV_BUF, 0), (0, 0)))
    row = lambda z: z.reshape(1, -1)
    vec = pl.BlockSpec((1, d), lambda i, j: (0, 0))
    out, tail = pl.pallas_call(
        _conv_prompt_body,
        grid=(b, t // tm),
        in_specs=[
            pl.BlockSpec((None, tm, d), lambda i, j: (i, j, 0)),
            pl.BlockSpec((None, CONV_HALO, d), lambda i, j: (i, 0, 0)),
            vec,
            pl.BlockSpec((d, d), lambda i, j: (0, 0)),
            pl.BlockSpec((d, d), lambda i, j: (0, 1)),
            pl.BlockSpec((1, d), lambda i, j: (0, 0)),
            pl.BlockSpec((1, d), lambda i, j: (0, 1)),
            pl.BlockSpec((CONV_WIDTH, d), lambda i, j: (0, 0)),
            vec, vec, vec,
            pl.BlockSpec((d, d), lambda i, j: (0, 0)),
            vec,
        ],
        out_specs=[pl.BlockSpec((None, tm, d), lambda i, j: (i, j, 0)),
                   pl.BlockSpec((None, CONV_HALO, d), lambda i, j: (i, 0, 0))],
        out_shape=[jax.ShapeDtypeStruct((b, t, d), F32), jax.ShapeDtypeStruct((b, CONV_HALO, d), F32)],
        scratch_shapes=[pltpu.VMEM((CONV_HALO + tm, d), F32)],
        compiler_params=_cparams("parallel", "arbitrary"),
        name="conv_prompt",
    )(x, bufp, row(norm_g), w_pw1, w_pw1, row(b_pw1), row(b_pw1), w_dw, row(b_dw), row(ln_g), row(ln_b),
      w_pw2, row(b_pw2))
    return out, tail[:, CONV_HALO - CONV_BUF:, :]


def _conv_sample_body(x_ref, ext_ref, wdw_ref, bdw_ref, lg_ref, lb_ref, w2_ref, b2_ref, o_ref):
    s = x_ref.shape[0]
    for t in range(s):
        c = jnp.zeros(x_ref.shape[1:], F32) + bdw_ref[...]
        for j in range(CONV_WIDTH):
            c = c + ext_ref[t + j] * wdw_ref[pl.ds(j, 1), :]
        z = _conv_tail(c, lg_ref[...], lb_ref[...]).astype(BF16)
        o_ref[t] = x_ref[t] + jnp.dot(z, w2_ref[...], preferred_element_type=F32) + b2_ref[...]


def _conv_sample(x, buf, norm_g, w_pw1, b_pw1, w_dw, b_dw, ln_g, ln_b, w_pw2, b_pw2, bb):
    b, s, d = x.shape
    u = _matmul(x.reshape(b * s, d), w_pw1, norm_g=norm_g, bias=b_pw1, glu=True).reshape(b, s, d)
    ext = jnp.concatenate([buf, u], axis=1)
    ext_t = jnp.transpose(ext, (1, 0, 2))
    x_t = jnp.transpose(x, (1, 0, 2))
    row = lambda z: z.reshape(1, -1)
    vec = pl.BlockSpec((1, d), lambda i: (0, 0))
    out_t = pl.pallas_call(
        _conv_sample_body,
        grid=(b // bb,),
        in_specs=[
            pl.BlockSpec((s, bb, d), lambda i: (0, i, 0)),
            pl.BlockSpec((CONV_BUF + s, bb, d), lambda i: (0, i, 0)),
            pl.BlockSpec((CONV_WIDTH, d), lambda i: (0, 0)),
            vec, vec, vec,
            pl.BlockSpec((d, d), lambda i: (0, 0)),
            vec,
        ],
        out_specs=pl.BlockSpec((s, bb, d), lambda i: (0, i, 0)),
        out_shape=jax.ShapeDtypeStruct((s, b, d), F32),
        compiler_params=_cparams("parallel"),
        name="conv_sample",
    )(x_t, ext_t, w_dw, row(b_dw), row(ln_g), row(ln_b), w_pw2, row(b_pw2))
    return jnp.transpose(out_t, (1, 0, 2)), ext[:, -CONV_BUF:, :]


def _lambda_value(lam_ref, lam_init):
    lp = lam_ref[...]
    l1 = jnp.sum(lp[0:1, :] * lp[1:2, :], axis=-1, keepdims=True)
    l2 = jnp.sum(lp[2:3, :] * lp[3:4, :], axis=-1, keepdims=True)
    return jnp.exp(l1) - jnp.exp(l2) + lam_init


def _split_maps(q):
    lane = lax.broadcasted_iota(jnp.int32, q.shape, q.ndim - 1) % ATT_VD
    return (jnp.where(lane < ATT_DH, q, 0.0).astype(BF16), jnp.where(lane >= ATT_DH, q, 0.0).astype(BF16))


def _flash_body(q_ref, k_ref, v_ref, lam_ref, sub_ref, o_ref, *, lam_init, tq):
    qi = pl.program_id(2)
    q1, q2 = _split_maps(q_ref[...] * ATT_SCALE)
    qs = (q1, q2)
    nt = (((1,), (1,)), ((), ()))

    def chunk(c, carry, masked):
        kc = k_ref[pl.ds(pl.multiple_of(c * tq, tq), tq), :].astype(BF16)
        vc = v_ref[pl.ds(pl.multiple_of(c * tq, tq), tq), :].astype(BF16)
        out = []
        for mp in range(2):
            m, l, acc = carry[mp]
            s = lax.dot_general(qs[mp], kc, nt, preferred_element_type=F32)
            if masked:
                row = lax.broadcasted_iota(jnp.int32, s.shape, 0)
                col = lax.broadcasted_iota(jnp.int32, s.shape, 1)
                s = jnp.where(col <= row, s, NEG_INF)
            m_new = jnp.maximum(m, jnp.max(s, axis=-1, keepdims=True))
            alpha = jnp.exp(m - m_new)
            p = jnp.exp(s - m_new)
            l = l * alpha + jnp.sum(p, axis=-1, keepdims=True)
            acc = acc * alpha + jnp.dot(p.astype(BF16), vc, preferred_element_type=F32)
            out.append((m_new, l, acc))
        return tuple(out)

    init = tuple((jnp.full((tq, 1), NEG_INF, F32), jnp.zeros((tq, 1), F32), jnp.zeros((tq, ATT_VD), F32))
                 for _ in range(2))
    carry = lax.fori_loop(0, qi, lambda c, cr: chunk(c, cr, False), init)
    carry = chunk(qi, carry, True)
    (_, l1, a1), (_, l2, a2) = carry
    lam = _lambda_value(lam_ref, lam_init)
    o = a1 / l1 - lam * (a2 / l2)
    o_ref[...] = _rms(o, sub_ref[...], SUBLN_EPS) * (1.0 - lam_init)


def _flash_prompt(qkv, b, t, lam_p, subln, lam_init, tq):
    d = qkv.shape[-1]
    return pl.pallas_call(
        functools.partial(_flash_body, lam_init=lam_init, tq=tq),
        grid=(b, ATT_HEADS, t // tq),
        in_specs=[
            pl.BlockSpec((None, None, tq, ATT_VD), lambda i, h, j: (0, i, j, h)),
            pl.BlockSpec((None, None, t, ATT_VD), lambda i, h, j: (1, i, 0, h)),
            pl.BlockSpec((None, None, t, ATT_VD), lambda i, h, j: (2, i, 0, h)),
            pl.BlockSpec((4, ATT_DH), lambda i, h, j: (0, 0)),
            pl.BlockSpec((1, ATT_VD), lambda i, h, j: (0, 0)),
        ],
        out_specs=pl.BlockSpec((None, tq, ATT_VD), lambda i, h, j: (i, j, h)),
        out_shape=jax.ShapeDtypeStruct((b, t, d), F32),
        compiler_params=_cparams("parallel", "parallel", "arbitrary"),
        name="flash_prompt",
    )(qkv, qkv, qkv, lam_p, subln.reshape(1, ATT_VD))


def _paged_body(*refs, lam_init, pages_per_step):
    pp = pages_per_step
    pt_ref = refs[0]
    q_ref, kn_ref, vn_ref = refs[1:4]
    kp_refs = refs[4:4 + pp]
    vp_refs = refs[4 + pp:4 + 2 * pp]
    lam_ref, sub_ref, o_ref, qm_ref, m_ref, l_ref, acc_ref = refs[4 + 2 * pp:]
    del pt_ref
    step = pl.program_id(1)
    s_len = o_ref.shape[0]
    sp = q_ref.shape[0]
    nt = (((1,), (1,)), ((), ()))

    @pl.when(step == 0)
    def _():
        q = q_ref[...] * ATT_SCALE
        qt = jnp.concatenate([q] * (ATT_HEADS * 2), axis=0)
        r = lax.broadcasted_iota(jnp.int32, qt.shape, 0)
        c = lax.broadcasted_iota(jnp.int32, qt.shape, 1)
        keep = jnp.logical_and(c // ATT_VD == r // (2 * sp), (c % ATT_VD) // ATT_DH == (r // sp) % 2)
        qm_ref[...] = jnp.where(keep, qt, 0.0).astype(BF16)
        m_ref[...] = jnp.full(m_ref.shape, NEG_INF, F32)
        l_ref[...] = jnp.zeros(l_ref.shape, F32)
        acc_ref[...] = jnp.zeros(acc_ref.shape, F32)

    qm = qm_ref[...]

    def merge(s, vb):
        m = m_ref[...]
        m_new = jnp.maximum(m, jnp.max(s, axis=-1, keepdims=True))
        alpha = jnp.exp(m - m_new)
        p = jnp.exp(s - m_new)
        l_ref[...] = l_ref[...] * alpha + jnp.sum(p, axis=-1, keepdims=True)
        acc_ref[...] = acc_ref[...] * alpha + jnp.dot(p.astype(BF16), vb, preferred_element_type=F32)
        m_ref[...] = m_new

    for kp_ref, vp_ref in zip(kp_refs, vp_refs):
        s = lax.dot_general(qm, kp_ref[...].astype(BF16), nt, preferred_element_type=F32)
        merge(s, vp_ref[...].astype(BF16))

    @pl.when(step == pl.num_programs(1) - 1)
    def _():
        s = lax.dot_general(qm, kn_ref[...].astype(BF16), nt, preferred_element_type=F32)
        r = lax.broadcasted_iota(jnp.int32, s.shape, 0) % sp
        c = lax.broadcasted_iota(jnp.int32, s.shape, 1)
        s = jnp.where(jnp.logical_and(c <= r, c < s_len), s, NEG_INF)
        merge(s, vn_ref[...].astype(BF16))
        o = acc_ref[...] / l_ref[...]
        lam = _lambda_value(lam_ref, lam_init)
        for h in range(ATT_HEADS):
            blk = o[h * 2 * sp:(h + 1) * 2 * sp, h * ATT_VD:(h + 1) * ATT_VD]
            oh = blk[:s_len] - lam * blk[sp:sp + s_len]
            o_ref[:, h * ATT_VD:(h + 1) * ATT_VD] = _rms(oh, sub_ref[...], SUBLN_EPS) * (1.0 - lam_init)


def _paged_sample(q, k_new, v_new, cache_k, cache_v, page_table, lam_p, subln, lam_init, pages_per_step):
    b, s, d = q.shape
    n_pages = page_table.shape[1]
    pp = pages_per_step
    sp = V7X_SUBLANES
    rows = ATT_HEADS * 2 * sp
    q, k_new, v_new = (jnp.pad(z, ((0, 0), (0, sp - s), (0, 0))) for z in (q, k_new, v_new))
    tok = pl.BlockSpec((None, sp, d), lambda i, j, pt: (i, 0, 0))

    def page_spec(o):
        return pl.BlockSpec((None, PAGE_SIZE, d), lambda i, j, pt: (pt[i, j * pp + o], 0, 0))

    grid_spec = pltpu.PrefetchScalarGridSpec(
        num_scalar_prefetch=1,
        grid=(b, n_pages // pp),
        in_specs=[tok, tok, tok] + [page_spec(o) for o in range(pp)] + [page_spec(o) for o in range(pp)] + [
            pl.BlockSpec((4, ATT_DH), lambda i, j, pt: (0, 0)),
            pl.BlockSpec((1, ATT_VD), lambda i, j, pt: (0, 0)),
        ],
        out_specs=pl.BlockSpec((None, s, d), lambda i, j, pt: (i, 0, 0)),
        scratch_shapes=[pltpu.VMEM((rows, d), BF16), pltpu.VMEM((rows, 1), F32), pltpu.VMEM((rows, 1), F32),
                        pltpu.VMEM((rows, d), F32)],
    )
    return pl.pallas_call(
        functools.partial(_paged_body, lam_init=lam_init, pages_per_step=pp),
        grid_spec=grid_spec,
        out_shape=jax.ShapeDtypeStruct((b, s, d), F32),
        compiler_params=_cparams("parallel", "arbitrary"),
        name="paged_sample",
    )(page_table, q, k_new, v_new, *([cache_k] * pp), *([cache_v] * pp), lam_p, subln.reshape(1, ATT_VD))


def kernel(x_prompt, x_sample, state_wkv, state_shift, state_conv, cache_k, cache_v, page_table, norm_mix, norm_ffn, norm_final, rwkv_mu, rwkv_w_rkv, rwkv_w0, rwkv_w1, rwkv_w2, rwkv_a0, rwkv_a1, rwkv_a2, rwkv_v0, rwkv_v1, rwkv_v2, rwkv_g1, rwkv_g2, rwkv_k_k, rwkv_k_a, rwkv_r_k, rwkv_ln_g, rwkv_ln_b, rwkv_w_o, conv_w_pw1, conv_b_pw1, conv_w_dw, conv_b_dw, conv_ln_g, conv_ln_b, conv_w_pw2, conv_b_pw2, attn_w_qkv, attn_lambda, attn_subln, attn_w_o, ffn_w13, ffn_w2, moe_router, moe_w13, moe_w2):
    d = D_MODEL
    bf = lambda z: z.astype(BF16)
    w_rkv, w1, w2, a1, a2, v1, v2, g1, g2, w_o = map(
        bf, (rwkv_w_rkv, rwkv_w1, rwkv_w2, rwkv_a1, rwkv_a2, rwkv_v1, rwkv_v2, rwkv_g1, rwkv_g2, rwkv_w_o))
    c_pw1, c_pw2 = bf(conv_w_pw1), bf(conv_w_pw2)
    a_qkv, a_wo = bf(attn_w_qkv), bf(attn_w_o)
    f_w13, f_w2 = bf(ffn_w13), bf(ffn_w2)
    m_w13, m_w2 = bf(moe_w13), bf(moe_w2)
    d_ff = ffn_w2.shape[1]
    n_pool = cache_k.shape[1]

    def run(x3, wkv_in, shift_in, conv_in, sample):
        b, t, _ = x3.shape
        m = b * t
        x = x3.reshape(m, d)
        wkv_out, shift_out, conv_out, k_out, v_out = [], [], [], [], []
        v_first = None
        for i in range(DEPTH):
            kind, j = i % 3, i // 3
            if kind == 0:
                vres = None if j == 0 else (rwkv_v0[j - 1], v1[j - 1], v2[j - 1])
                x, sh, s_new, v_first = _rwkv_layer(
                    x, b, t, shift_in[j], wkv_in[j], v_first, vres, norm_mix[i], rwkv_mu[j], w_rkv[j],
                    rwkv_w0[j], w1[j], w2[j], rwkv_a0[j], a1[j], a2[j], g1[j], g2[j], rwkv_k_k[j],
                    rwkv_k_a[j], rwkv_r_k[j], rwkv_ln_g[j], rwkv_ln_b[j], w_o[j])
                wkv_out.append(s_new)
                shift_out.append(sh)
            elif kind == 1:
                conv_args = (norm_mix[i], c_pw1[j], conv_b_pw1[j], conv_w_dw[j], conv_b_dw[j], conv_ln_g[j],
                             conv_ln_b[j], c_pw2[j], conv_b_pw2[j])
                if sample:
                    y3, buf = _conv_sample(x.reshape(b, t, d), conv_in[j], *conv_args, 32)
                else:
                    y3, buf = _conv_prompt(x.reshape(b, t, d), conv_in[j], *conv_args, 256)
                x = y3.reshape(m, d)
                conv_out.append(buf)
            else:
                lam_init = 0.8 - 0.6 * math.exp(-0.3 * i)
                qkv = _matmul(x, a_qkv[j], norm_g=norm_mix[i], col_tile=d, split_out=True)
                if sample:
                    q3, k3, v3 = (qkv[n].reshape(b, t, d) for n in range(3))
                    o = _paged_sample(q3, k3, v3, cache_k[j].reshape(n_pool, PAGE_SIZE, d),
                                      cache_v[j].reshape(n_pool, PAGE_SIZE, d), page_table, attn_lambda[j],
                                      attn_subln[j], lam_init, 4)
                else:
                    o = _flash_prompt(qkv.reshape(3, b, t, d), b, t, attn_lambda[j], attn_subln[j], lam_init, 256)
                x = _matmul(o.reshape(m, d), a_wo[j], res=x)
                k_out.append(qkv[1].reshape(b, t, ATT_HEADS, 2 * ATT_DH))
                v_out.append(qkv[2].reshape(b, t, ATT_HEADS, ATT_VD))
            if i % 2 == 0:
                x = _ffn_dense(x, norm_ffn[i], f_w13[i // 2], f_w2[i // 2], d_ff // 2)
            else:
                x = _moe(x, norm_ffn[i], moe_router[i // 2], m_w13[i // 2], m_w2[i // 2], 512)
        y = _rmsnorm(x, norm_final).reshape(b, t, d)
        return (y, jnp.stack(wkv_out), jnp.stack(shift_out), jnp.stack(conv_out), jnp.stack(k_out),
                jnp.stack(v_out))

    bp = x_prompt.shape[0]
    wkv0 = jnp.zeros((state_wkv.shape[0], bp) + state_wkv.shape[2:], state_wkv.dtype)
    shift0 = jnp.zeros((state_shift.shape[0], bp, d), x_prompt.dtype)
    conv0 = jnp.zeros((state_conv.shape[0], bp, CONV_BUF, d), x_prompt.dtype)
    y_p, wkv_p, sh_p, conv_p, k_p, v_p = run(x_prompt, wkv0, shift0, conv0, False)
    y_s, wkv_s, sh_s, conv_s, k_s, v_s = run(x_sample, state_wkv, state_shift, state_conv, True)
    return (y_p, y_s, wkv_p, wkv_s, sh_p, sh_s, conv_p, conv_s, k_p, k_s, v_p, v_s)
```

```python
import functools
import math

import jax
import jax.numpy as jnp
from jax import lax
from jax.experimental import pallas as pl
from jax.experimental.pallas import tpu as pltpu

F32 = jnp.float32
BF16 = jnp.bfloat16

D_MODEL = 1024
DEPTH = 4
PAGE_SIZE = 128
RWKV_HEAD = 64
RWKV_HEADS = D_MODEL // RWKV_HEAD
GN_EPS = 64e-5
CONV_WIDTH = 31
CONV_BUF = CONV_WIDTH - 1
LN_EPS = 1e-5
ATT_HEADS = 8
ATT_DH = 64
ATT_VD = 128
ATT_SCALE = ATT_DH ** -0.5
SUBLN_EPS = 1e-5
N_EXPERTS = 8
TOP_K = 2
RMS_EPS = 1e-6

V7X_LANES = 128
V7X_SUBLANES = 8
V7X_VMEM_LIMIT_BYTES = 56 * 1024 * 1024

ROW_TILE = 512
CONV_HALO = 32
NEG_INF = float("-inf")


def _cparams(*sem):
    return pltpu.CompilerParams(dimension_semantics=sem, vmem_limit_bytes=V7X_VMEM_LIMIT_BYTES)


def _rms(x, g, eps=RMS_EPS):
    return x * lax.rsqrt(jnp.mean(x * x, axis=-1, keepdims=True) + eps) * g


def _sigmoid(x):
    return 1.0 / (1.0 + jnp.exp(-x))


def _silu(x):
    return x * _sigmoid(x)


def _row_tile(m):
    return min(ROW_TILE, m)


def _rms_body(x_ref, g_ref, o_ref):
    o_ref[...] = _rms(x_ref[...], g_ref[...]).astype(o_ref.dtype)


def _rmsnorm(x, g, out_dtype=F32):
    m, d = x.shape
    tm = _row_tile(m)
    return pl.pallas_call(
        _rms_body,
        grid=(m // tm,),
        in_specs=[pl.BlockSpec((tm, d), lambda i: (i, 0)), pl.BlockSpec((1, d), lambda i: (0, 0))],
        out_specs=pl.BlockSpec((tm, d), lambda i: (i, 0)),
        out_shape=jax.ShapeDtypeStruct((m, d), out_dtype),
        compiler_params=_cparams("parallel"),
        name="rmsnorm",
    )(x, g.reshape(1, d))


def _mm_body(*refs, norm, mul, bias, act, res, glu, bf16_copy):
    it = iter(refs)
    x_ref = next(it)
    g_ref = next(it) if norm else None
    m_ref = next(it) if mul else None
    w_ref = next(it)
    w2_ref = next(it) if glu else None
    b_ref = next(it) if bias else None
    b2_ref = next(it) if (bias and glu) else None
    r_ref = next(it) if res else None
    o_ref = next(it)
    ob_ref = next(it) if bf16_copy else None
    xs_ref = next(it)

    @pl.when(pl.program_id(1) == 0)
    def _():
        x = x_ref[...].astype(F32)
        if norm:
            x = _rms(x, g_ref[...])
        if mul:
            x = x * m_ref[...]
        xs_ref[...] = x.astype(BF16)

    xs = xs_ref[...]
    y = jnp.dot(xs, w_ref[...], preferred_element_type=F32)
    if bias:
        y = y + b_ref[...]
    if glu:
        y2 = jnp.dot(xs, w2_ref[...], preferred_element_type=F32)
        if bias:
            y2 = y2 + b2_ref[...]
        y = y * _sigmoid(y2)
    if act == "tanh":
        y = jnp.tanh(y)
    elif act == "sigmoid":
        y = _sigmoid(y)
    if res:
        y = y + r_ref[...]
    o_ref[...] = y.astype(o_ref.dtype)
    if bf16_copy:
        ob_ref[...] = y.astype(BF16)


def _matmul(x, w, *, norm_g=None, mul=None, bias=None, act=None, res=None, glu=False,
            col_tile=None, split_out=False, bf16_copy=False):
    m, k = x.shape
    n = w.shape[1]
    n_out = n // 2 if glu else n
    tm = _row_tile(m)
    tn = col_tile or n_out
    nj = n_out // tn
    in_specs = [pl.BlockSpec((tm, k), lambda i, j: (i, 0))]
    args = [x]
    if norm_g is not None:
        in_specs.append(pl.BlockSpec((1, k), lambda i, j: (0, 0)))
        args.append(norm_g.reshape(1, k))
    if mul is not None:
        in_specs.append(pl.BlockSpec((tm, k), lambda i, j: (i, 0)))
        args.append(mul)
    in_specs.append(pl.BlockSpec((k, tn), lambda i, j: (0, j)))
    args.append(w)
    if glu:
        in_specs.append(pl.BlockSpec((k, tn), lambda i, j: (0, j + nj)))
        args.append(w)
    if bias is not None:
        b2d = bias.reshape(1, n)
        in_specs.append(pl.BlockSpec((1, tn), lambda i, j: (0, j)))
        args.append(b2d)
        if glu:
            in_specs.append(pl.BlockSpec((1, tn), lambda i, j: (0, j + nj)))
            args.append(b2d)
    if res is not None:
        in_specs.append(pl.BlockSpec((tm, tn), lambda i, j: (i, j)))
        args.append(res)
    if split_out:
        out_shape = jax.ShapeDtypeStruct((nj, m, tn), F32)
        out_spec = pl.BlockSpec((None, tm, tn), lambda i, j: (j, i, 0))
    else:
        out_shape = jax.ShapeDtypeStruct((m, n_out), F32)
        out_spec = pl.BlockSpec((tm, tn), lambda i, j: (i, j))
    if bf16_copy:
        out_shape = [out_shape, jax.ShapeDtypeStruct(out_shape.shape, BF16)]
        out_spec = [out_spec, out_spec]
    body = functools.partial(_mm_body, norm=norm_g is not None, mul=mul is not None,
                             bias=bias is not None, act=act, res=res is not None, glu=glu,
                             bf16_copy=bf16_copy)
    return pl.pallas_call(
        body,
        grid=(m // tm, nj),
        in_specs=in_specs,
        out_specs=out_spec,
        out_shape=out_shape,
        scratch_shapes=[pltpu.VMEM((tm, k), BF16)],
        compiler_params=_cparams("parallel", "arbitrary"),
        name="matmul",
    )(*args)


def _rows_from_tiles(ref):
    return jnp.concatenate([ref[:, s, :] for s in range(ref.shape[1])], axis=-1)


def _rows_to_tiles(ref, val):
    for s in range(ref.shape[1]):
        ref[:, s, :] = val[:, s * V7X_LANES:(s + 1) * V7X_LANES]


def _ffn_body(*refs, moe):
    if moe:
        te_ref, tv_ref, x_ref, g_ref, wg_ref, wu_ref, wd_ref, o_ref, hs_ref, acc_ref = refs
    else:
        x_ref, g_ref, wg_ref, wu_ref, wd_ref, o_ref, hs_ref, acc_ref = refs
    j = pl.program_id(1)
    nj = pl.num_programs(1)

    def compute():
        @pl.when(j == 0)
        def _():
            x = _rows_from_tiles(x_ref) if moe else x_ref[...]
            hs_ref[...] = _rms(x, g_ref[...]).astype(BF16)
            acc_ref[...] = jnp.zeros_like(acc_ref)

        h = hs_ref[...]
        gate = jnp.dot(h, wg_ref[...], preferred_element_type=F32)
        up = jnp.dot(h, wu_ref[...], preferred_element_type=F32)
        a = (_silu(gate) * up).astype(BF16)
        acc_ref[...] += jnp.dot(a, wd_ref[...], preferred_element_type=F32)

        @pl.when(j == nj - 1)
        def _():
            if moe:
                _rows_to_tiles(o_ref, acc_ref[...])
            else:
                o_ref[...] = x_ref[...] + acc_ref[...]

    if moe:
        valid = tv_ref[pl.program_id(0)] > 0
        pl.when(valid)(compute)

        @pl.when(jnp.logical_and(jnp.logical_not(valid), j == nj - 1))
        def _():
            o_ref[...] = jnp.zeros_like(o_ref)
    else:
        compute()


def _ffn_dense(x, g, w13, w2, ff_tile):
    m, d = x.shape
    f = w2.shape[0]
    tm = _row_tile(m)
    nf = f // ff_tile
    return pl.pallas_call(
        functools.partial(_ffn_body, moe=False),
        grid=(m // tm, nf),
        in_specs=[
            pl.BlockSpec((tm, d), lambda i, j: (i, 0)),
            pl.BlockSpec((1, d), lambda i, j: (0, 0)),
            pl.BlockSpec((d, ff_tile), lambda i, j: (0, j)),
            pl.BlockSpec((d, ff_tile), lambda i, j: (0, j + nf)),
            pl.BlockSpec((ff_tile, d), lambda i, j: (j, 0)),
        ],
        out_specs=pl.BlockSpec((tm, d), lambda i, j: (i, 0)),
        out_shape=jax.ShapeDtypeStruct((m, d), F32),
        scratch_shapes=[pltpu.VMEM((tm, d), BF16), pltpu.VMEM((tm, d), F32)],
        compiler_params=_cparams("parallel", "arbitrary"),
        name="ffn_dense",
    )(x, g.reshape(1, d), w13, w13, w2)


def _ffn_experts(xs, g, w13, w2, tile_expert, tile_valid, tm, ff_tile):
    s, sub, lanes = xs.shape
    d = sub * lanes
    f = w2.shape[1]
    nf = f // ff_tile
    tok = pl.BlockSpec((tm, sub, lanes), lambda i, j, te, tv: (i, 0, 0))
    grid_spec = pltpu.PrefetchScalarGridSpec(
        num_scalar_prefetch=2,
        grid=(s // tm, nf),
        in_specs=[
            tok,
            pl.BlockSpec((1, d), lambda i, j, te, tv: (0, 0)),
            pl.BlockSpec((None, d, ff_tile), lambda i, j, te, tv: (te[i], 0, j)),
            pl.BlockSpec((None, d, ff_tile), lambda i, j, te, tv: (te[i], 0, j + nf)),
            pl.BlockSpec((None, ff_tile, d), lambda i, j, te, tv: (te[i], j, 0)),
        ],
        out_specs=tok,
        scratch_shapes=[pltpu.VMEM((tm, d), BF16), pltpu.VMEM((tm, d), F32)],
    )
    return pl.pallas_call(
        functools.partial(_ffn_body, moe=True),
        grid_spec=grid_spec,
        out_shape=jax.ShapeDtypeStruct((s, sub, lanes), F32),
        compiler_params=_cparams("parallel", "arbitrary"),
        name="ffn_experts",
    )(tile_expert, tile_valid, xs, g.reshape(1, d), w13, w13, w2)


def _router_body(x_ref, g_ref, whi_ref, wlo_ref, o_ref, g1_ref, g2_ref):
    h = _rms(x_ref[...], g_ref[...])
    hhi = h.astype(BF16)
    hlo = (h - hhi.astype(F32)).astype(BF16)
    whi = whi_ref[...]
    lg = (jnp.dot(hhi, whi, preferred_element_type=F32)
          + jnp.dot(hlo, whi, preferred_element_type=F32)
          + jnp.dot(hhi, wlo_ref[...], preferred_element_type=F32))
    lane = lax.broadcasted_iota(jnp.int32, lg.shape, 1)
    lg = jnp.where(lane < N_EXPERTS, lg, NEG_INF)
    m1 = jnp.max(lg, axis=-1, keepdims=True)
    i1 = jnp.min(jnp.where(lg == m1, lane, V7X_LANES), axis=-1, keepdims=True)
    lg2 = jnp.where(lane == i1, NEG_INF, lg)
    m2 = jnp.max(lg2, axis=-1, keepdims=True)
    i2 = jnp.min(jnp.where(lg2 == m2, lane, V7X_LANES), axis=-1, keepdims=True)
    e2 = jnp.exp(m2 - m1)
    g1 = 1.0 / (1.0 + e2)
    g2 = e2 / (1.0 + e2)
    out = jnp.where(lane == 0, i1.astype(F32),
                    jnp.where(lane == 1, i2.astype(F32),
                              jnp.where(lane == 2, g1, jnp.where(lane == 3, g2, 0.0))))
    o_ref[...] = out
    g1_ref[:, 0, :] = jnp.broadcast_to(g1, lg.shape)
    g2_ref[:, 0, :] = jnp.broadcast_to(g2, lg.shape)


def _router(x, g, w_router):
    m, d = x.shape
    tm = _row_tile(m)
    wpad = jnp.zeros((d, V7X_LANES), F32).at[:, :N_EXPERTS].set(w_router)
    whi = wpad.astype(BF16)
    wlo = (wpad - whi.astype(F32)).astype(BF16)
    gate_spec = pl.BlockSpec((tm, 1, V7X_LANES), lambda i: (i, 0, 0))
    gate_shape = jax.ShapeDtypeStruct((m, 1, V7X_LANES), F32)
    return pl.pallas_call(
        _router_body,
        grid=(m // tm,),
        in_specs=[
            pl.BlockSpec((tm, d), lambda i: (i, 0)),
            pl.BlockSpec((1, d), lambda i: (0, 0)),
            pl.BlockSpec((d, V7X_LANES), lambda i: (0, 0)),
            pl.BlockSpec((d, V7X_LANES), lambda i: (0, 0)),
        ],
        out_specs=[pl.BlockSpec((tm, V7X_LANES), lambda i: (i, 0)), gate_spec, gate_spec],
        out_shape=[jax.ShapeDtypeStruct((m, V7X_LANES), F32), gate_shape, gate_shape],
        compiler_params=_cparams("parallel"),
        name="router",
    )(x, g.reshape(1, d), whi, wlo)


DMA_UNROLL = 8


def _gather_body(src_ref, x_hbm, o_ref, sem):
    tg = o_ref.shape[0]
    base = pl.program_id(0) * tg

    def tile_copy(r):
        return pltpu.make_async_copy(x_hbm.at[pl.ds(src_ref[base + r], 1)], o_ref.at[pl.ds(r, 1)], sem)

    def start(i, c):
        for u in range(DMA_UNROLL):
            tile_copy(i * DMA_UNROLL + u).start(priority=u % 2)
        return c

    def wait(i, c):
        for u in range(DMA_UNROLL):
            tile_copy(i * DMA_UNROLL + u).wait()
        return c

    lax.fori_loop(0, tg // DMA_UNROLL, start, 0)
    lax.fori_loop(0, tg // DMA_UNROLL, wait, 0)


def _gather_tokens(x3, src, tg):
    s = src.shape[0]
    _, sub, lanes = x3.shape
    grid_spec = pltpu.PrefetchScalarGridSpec(
        num_scalar_prefetch=1,
        grid=(s // tg,),
        in_specs=[pl.BlockSpec(memory_space=pl.ANY)],
        out_specs=pl.BlockSpec((tg, sub, lanes), lambda i, src: (i, 0, 0)),
        scratch_shapes=[pltpu.SemaphoreType.DMA(())],
    )
    return pl.pallas_call(
        _gather_body,
        grid_spec=grid_spec,
        out_shape=jax.ShapeDtypeStruct((s, sub, lanes), x3.dtype),
        compiler_params=_cparams("arbitrary"),
        name="moe_gather",
    )(src, x3)


def _combine_body(dest_ref, x_ref, g1_ref, g2_ref, y_hbm, o_ref, buf, sem):
    tc = o_ref.shape[0]
    base = pl.program_id(0) * tc

    def tile_copy(r, k):
        return pltpu.make_async_copy(y_hbm.at[pl.ds(dest_ref[TOP_K * (base + r) + k], 1)],
                                     buf.at[k, pl.ds(r, 1)], sem)

    def start(i, c):
        for u in range(DMA_UNROLL):
            for k in range(TOP_K):
                tile_copy(i * DMA_UNROLL + u, k).start(priority=k)
        return c

    def wait(i, c):
        for u in range(DMA_UNROLL):
            for k in range(TOP_K):
                tile_copy(i * DMA_UNROLL + u, k).wait()
        return c

    lax.fori_loop(0, tc // DMA_UNROLL, start, 0)
    lax.fori_loop(0, tc // DMA_UNROLL, wait, 0)
    o_ref[...] = x_ref[...] + g1_ref[...] * buf[0] + g2_ref[...] * buf[1]


def _moe_combine(x3, g1, g2, y3, dest, tc):
    m, sub, lanes = x3.shape
    tok = pl.BlockSpec((tc, sub, lanes), lambda i, dest: (i, 0, 0))
    gate = pl.BlockSpec((tc, 1, lanes), lambda i, dest: (i, 0, 0))
    grid_spec = pltpu.PrefetchScalarGridSpec(
        num_scalar_prefetch=1,
        grid=(m // tc,),
        in_specs=[tok, gate, gate, pl.BlockSpec(memory_space=pl.ANY)],
        out_specs=tok,
        scratch_shapes=[pltpu.VMEM((TOP_K, tc, sub, lanes), F32), pltpu.SemaphoreType.DMA(())],
    )
    return pl.pallas_call(
        _combine_body,
        grid_spec=grid_spec,
        out_shape=jax.ShapeDtypeStruct((m, sub, lanes), F32),
        compiler_params=_cparams("arbitrary"),
        name="moe_combine",
    )(dest, x3, g1, g2, y3)


def _moe(x, g, w_router, w13, w2, ff_tile):
    m, d = x.shape
    tm = _row_tile(m)
    r, g1, g2 = _router(x, g, w_router)
    e_flat = r[:, :TOP_K].astype(jnp.int32).reshape(-1)
    n_slots = m * TOP_K
    n_tiles = n_slots // tm + N_EXPERTS
    s_pad = n_tiles * tm
    onehot = (e_flat[:, None] == jnp.arange(N_EXPERTS, dtype=jnp.int32)[None, :]).astype(jnp.int32)
    rank = jnp.sum((jnp.cumsum(onehot, axis=0) - 1) * onehot, axis=1)
    counts = jnp.sum(onehot, axis=0)
    padded = ((counts + tm - 1) // tm) * tm
    ends = jnp.cumsum(padded)
    starts = ends - padded
    dest = (jnp.sum(onehot * starts[None, :], axis=1) + rank).astype(jnp.int32)
    token = jnp.arange(n_slots, dtype=jnp.int32) // TOP_K
    src = jnp.zeros((s_pad,), jnp.int32).at[dest].set(token)
    tile_start = jnp.arange(n_tiles, dtype=jnp.int32) * tm
    tile_valid = (tile_start < ends[-1]).astype(jnp.int32)
    tile_expert = jnp.sum((tile_start[:, None] >= ends[None, :]).astype(jnp.int32), axis=1)
    last_expert = jnp.max(jnp.where(tile_valid > 0, tile_expert, 0))
    tile_expert = jnp.where(tile_valid > 0, tile_expert, last_expert).astype(jnp.int32)
    x3 = x.reshape(m, d // V7X_LANES, V7X_LANES)
    xs = _gather_tokens(x3, src, tm)
    ys = _ffn_experts(xs, g, w13, w2, tile_expert, tile_valid, tm, ff_tile)
    return _moe_combine(x3, g1, g2, ys, dest, min(256, m)).reshape(m, d)


def _wkv_body(r_ref, wl_ref, k_ref, v_ref, a_ref, kk_ref, ka_ref, rk_ref, lg_ref, lb_ref, s0_ref,
              y_ref, s_ref, w_s, a_s, b_s, k_s, r_s):
    tc = pl.program_id(1)
    n = RWKV_HEAD
    tt = r_ref.shape[0]

    @pl.when(tc == 0)
    def _():
        s_ref[...] = s0_ref[...]

    kkw = kk_ref[...]
    kaw = ka_ref[...]
    rkw = rk_ref[...]
    lgw = lg_ref[...]
    lbw = lb_ref[...]

    def step(t, carry):
        r = r_ref[t]
        k = k_ref[t]
        v = v_ref[t]
        a = a_ref[t]
        w = jnp.exp(-jnp.exp(wl_ref[t]))
        kk = k * kkw
        kk = kk / jnp.maximum(jnp.sqrt(jnp.sum(kk * kk, axis=0, keepdims=True)), 1e-12)
        k2 = k * (1.0 + (a - 1.0) * kaw)
        w_s[...] = w
        a_s[...] = -kk
        b_s[...] = kk * a
        k_s[...] = k2
        r_s[...] = r

        def bc(ref, i):
            return jnp.broadcast_to(ref[pl.ds(i, 1), :], (n, V7X_LANES))

        sa = jnp.zeros((n, V7X_LANES), F32)
        for i in range(n):
            sa = sa + s_ref[i] * bc(a_s, i)
        y = jnp.zeros((n, V7X_LANES), F32)
        for i in range(n):
            s_new = s_ref[i] * bc(w_s, i) + sa * bc(b_s, i) + v * bc(k_s, i)
            s_ref[i] = s_new
            y = y + s_new * bc(r_s, i)
        ym = jnp.mean(y, axis=0, keepdims=True)
        yc = y - ym
        yn = yc * lax.rsqrt(jnp.mean(yc * yc, axis=0, keepdims=True) + GN_EPS)
        bonus = jnp.sum(r * k2 * rkw, axis=0, keepdims=True) * v
        y_ref[t] = yn * lgw + lbw + bonus
        return carry

    lax.fori_loop(0, tt, step, 0)


def _wkv(r, wl, k, v, a, k_k, k_a, r_k, ln_g, ln_b, s0, time_tile):
    t, n, l = r.shape
    tt = min(time_tile, t)
    seq = pl.BlockSpec((tt, n, V7X_LANES), lambda i, j: (j, 0, i))
    par = pl.BlockSpec((n, V7X_LANES), lambda i, j: (0, i))
    st = pl.BlockSpec((n, n, V7X_LANES), lambda i, j: (0, 0, i))
    return pl.pallas_call(
        _wkv_body,
        grid=(l // V7X_LANES, t // tt),
        in_specs=[seq, seq, seq, seq, seq, par, par, par, par, par, st],
        out_specs=[seq, st],
        out_shape=[jax.ShapeDtypeStruct((t, n, l), F32), jax.ShapeDtypeStruct((n, n, l), F32)],
        scratch_shapes=[pltpu.VMEM((n, V7X_LANES), F32)] * 5,
        compiler_params=_cparams("parallel", "arbitrary"),
        name="wkv7",
    )(r, wl, k, v, a, k_k, k_a, r_k, ln_g, ln_b, s0)


def _softplus(z):
    return jnp.maximum(z, 0.0) + jnp.log1p(jnp.exp(-jnp.abs(z)))


def _rwkv_proj_body(*refs, has_vres):
    (h_ref, hp_ref, mu_ref, wr_ref, wk_ref, wv_ref, w0_ref, w1_ref, w2_ref,
     a0_ref, a1_ref, a2_ref, g1_ref, g2_ref) = refs[:14]
    n_in = 14
    if has_vres:
        vf_ref, v0_ref, v1_ref, v2_ref = refs[14:18]
        n_in = 18
    r_ref, k_ref, v_ref, wl_ref, a_ref, g_ref = refs[n_in:]
    h = h_ref[...]
    xx = hp_ref[...] - h

    def mix(i):
        return (h + xx * mu_ref[pl.ds(i, 1), :]).astype(BF16)

    def dot(x, w_ref):
        return jnp.dot(x.astype(BF16), w_ref[...], preferred_element_type=F32)

    r_ref[...] = dot(mix(0), wr_ref)
    z = w0_ref[...] + dot(jnp.tanh(dot(mix(1), w1_ref)), w2_ref)
    wl_ref[...] = -_softplus(-z) - 0.5
    k_ref[...] = dot(mix(2), wk_ref)
    xv = mix(3)
    v = dot(xv, wv_ref)
    if has_vres:
        v = v + (vf_ref[...] - v) * _sigmoid(v0_ref[...] + dot(dot(xv, v1_ref), v2_ref))
    v_ref[...] = v
    a_ref[...] = _sigmoid(a0_ref[...] + dot(dot(mix(4), a1_ref), a2_ref))
    g_ref[...] = dot(_sigmoid(dot(mix(5), g1_ref)), g2_ref)


def _rwkv_proj(h, h_prev, mu, w_rkv, w0, w1, w2, a0, a1, a2, g1, g2, v_first, vres):
    m, c = h.shape
    tm = min(256, m)
    tok = pl.BlockSpec((tm, c), lambda i: (i, 0))

    def whole(z):
        return pl.BlockSpec(z.shape, lambda i: (0,) * z.ndim)

    row = lambda z: z.reshape(1, c)
    consts = [mu, w_rkv[0], w_rkv[1], w_rkv[2], row(w0), w1, w2, row(a0), a1, a2, g1, g2]
    args = [h, h_prev] + consts
    in_specs = [tok, tok] + [whole(z) for z in consts]
    if vres is not None:
        v0, v1, v2 = vres
        extra = [row(v0), v1, v2]
        args += [v_first] + extra
        in_specs += [tok] + [whole(z) for z in extra]
    return pl.pallas_call(
        functools.partial(_rwkv_proj_body, has_vres=vres is not None),
        grid=(m // tm,),
        in_specs=in_specs,
        out_specs=[tok] * 6,
        out_shape=[jax.ShapeDtypeStruct((m, c), F32)] * 6,
        compiler_params=_cparams("parallel"),
        name="rwkv_proj",
    )(*args)


def _to_lanes(z, b, t):
    return jnp.transpose(z.reshape(b, t, RWKV_HEADS, RWKV_HEAD), (1, 3, 0, 2)).reshape(t, RWKV_HEAD, b * RWKV_HEADS)


def _head_param(p, b):
    ph = p.reshape(RWKV_HEADS, RWKV_HEAD).T
    return jnp.tile(ph, (1, b))


def _rwkv_layer(x, b, t, shift_prev, s0, v_first, vres, norm_g, mu, w_rkv, w0, w1, w2, a0, a1, a2,
                g1, g2, k_k, k_a, r_k, ln_g, ln_b, w_o):
    m, c = x.shape
    h = _rmsnorm(x, norm_g)
    h3 = h.reshape(b, t, c)
    h_prev = jnp.concatenate([shift_prev[:, None, :], h3[:, :-1]], axis=1)
    r, k, v, wl, a, g = _rwkv_proj(h, h_prev.reshape(m, c), mu, w_rkv, w0, w1, w2, a0, a1, a2, g1, g2,
                                   v_first, vres)
    if vres is None:
        v_first = v
    tl = lambda z: _to_lanes(z, b, t)
    hp = lambda p: _head_param(p, b)
    s0_l = jnp.transpose(s0, (3, 2, 0, 1)).reshape(RWKV_HEAD, RWKV_HEAD, b * RWKV_HEADS)
    y_l, s_l = _wkv(tl(r), tl(wl), tl(k), tl(v), tl(a), hp(k_k), hp(k_a), hp(r_k), hp(ln_g), hp(ln_b),
                    s0_l, 32)
    y = jnp.transpose(y_l.reshape(t, RWKV_HEAD, b, RWKV_HEADS), (2, 0, 3, 1)).reshape(m, c)
    s_new = jnp.transpose(s_l.reshape(RWKV_HEAD, RWKV_HEAD, b, RWKV_HEADS), (2, 3, 1, 0))
    out = _matmul(y, w_o, mul=g, res=x)
    return out, h3[:, -1], s_new, v_first


def _conv_tail(c, ln_g, ln_b):
    mu = jnp.mean(c, axis=-1, keepdims=True)
    xc = c - mu
    var = jnp.mean(xc * xc, axis=-1, keepdims=True)
    return _silu(xc * lax.rsqrt(var + LN_EPS) * ln_g + ln_b)


def _conv_prompt_body(x_ref, buf_ref, g_ref, wa_ref, wb_ref, ba_ref, bb_ref, wdw_ref, bdw_ref,
                      lg_ref, lb_ref, w2_ref, b2_ref, o_ref, tail_ref, ext_ref):
    ti = pl.program_id(1)
    tm = x_ref.shape[0]

    @pl.when(ti == 0)
    def _():
        ext_ref[pl.ds(0, CONV_HALO), :] = buf_ref[...]

    x = x_ref[...]
    h = _rms(x, g_ref[...]).astype(BF16)
    ua = jnp.dot(h, wa_ref[...], preferred_element_type=F32) + ba_ref[...]
    ub = jnp.dot(h, wb_ref[...], preferred_element_type=F32) + bb_ref[...]
    ext_ref[pl.ds(CONV_HALO, tm), :] = ua * _sigmoid(ub)
    c = jnp.zeros_like(x) + bdw_ref[...]
    for j in range(CONV_WIDTH):
        c = c + ext_ref[pl.ds(CONV_HALO - CONV_BUF + j, tm), :] * wdw_ref[pl.ds(j, 1), :]
    z = _conv_tail(c, lg_ref[...], lb_ref[...]).astype(BF16)
    o_ref[...] = x + jnp.dot(z, w2_ref[...], preferred_element_type=F32) + b2_ref[...]
    tail = ext_ref[pl.ds(tm, CONV_HALO), :]
    ext_ref[pl.ds(0, CONV_HALO), :] = tail

    @pl.when(ti == pl.num_programs(1) - 1)
    def _():
        tail_ref[...] = tail


def _conv_prompt(x, buf, norm_g, w_pw1, b_pw1, w_dw, b_dw, ln_g, ln_b, w_pw2, b_pw2, tm):
    b, t, d = x.shape
    bufp = jnp.pad(buf, ((0, 0), (CONV_HALO - CONV_BUF, 0), (0, 0)))
    row = lambda z: z.reshape(1, -1)
    vec = pl.BlockSpec((1, d), lambda i, j: (0, 0))
    out, tail = pl.pallas_call(
        _conv_prompt_body,
        grid=(b, t // tm),
        in_specs=[
            pl.BlockSpec((None, tm, d), lambda i, j: (i, j, 0)),
            pl.BlockSpec((None, CONV_HALO, d), lambda i, j: (i, 0, 0)),
            vec,
            pl.BlockSpec((d, d), lambda i, j: (0, 0)),
            pl.BlockSpec((d, d), lambda i, j: (0, 1)),
            pl.BlockSpec((1, d), lambda i, j: (0, 0)),
            pl.BlockSpec((1, d), lambda i, j: (0, 1)),
            pl.BlockSpec((CONV_WIDTH, d), lambda i, j: (0, 0)),
            vec, vec, vec,
            pl.BlockSpec((d, d), lambda i, j: (0, 0)),
            vec,
        ],
        out_specs=[pl.BlockSpec((None, tm, d), lambda i, j: (i, j, 0)),
                   pl.BlockSpec((None, CONV_HALO, d), lambda i, j: (i, 0, 0))],
        out_shape=[jax.ShapeDtypeStruct((b, t, d), F32), jax.ShapeDtypeStruct((b, CONV_HALO, d), F32)],
        scratch_shapes=[pltpu.VMEM((CONV_HALO + tm, d), F32)],
        compiler_params=_cparams("parallel", "arbitrary"),
        name="conv_prompt",
    )(x, bufp, row(norm_g), w_pw1, w_pw1, row(b_pw1), row(b_pw1), w_dw, row(b_dw), row(ln_g), row(ln_b),
      w_pw2, row(b_pw2))
    return out, tail[:, CONV_HALO - CONV_BUF:, :]


def _conv_sample_body(x_ref, ext_ref, wdw_ref, bdw_ref, lg_ref, lb_ref, w2_ref, b2_ref, o_ref):
    s = x_ref.shape[0]
    for t in range(s):
        c = jnp.zeros(x_ref.shape[1:], F32) + bdw_ref[...]
        for j in range(CONV_WIDTH):
            c = c + ext_ref[t + j] * wdw_ref[pl.ds(j, 1), :]
        z = _conv_tail(c, lg_ref[...], lb_ref[...]).astype(BF16)
        o_ref[t] = x_ref[t] + jnp.dot(z, w2_ref[...], preferred_element_type=F32) + b2_ref[...]


def _conv_sample(x, buf, norm_g, w_pw1, b_pw1, w_dw, b_dw, ln_g, ln_b, w_pw2, b_pw2, bb):
    b, s, d = x.shape
    u = _matmul(x.reshape(b * s, d), w_pw1, norm_g=norm_g, bias=b_pw1, glu=True).reshape(b, s, d)
    ext = jnp.concatenate([buf, u], axis=1)
    ext_t = jnp.transpose(ext, (1, 0, 2))
    x_t = jnp.transpose(x, (1, 0, 2))
    row = lambda z: z.reshape(1, -1)
    vec = pl.BlockSpec((1, d), lambda i: (0, 0))
    out_t = pl.pallas_call(
        _conv_sample_body,
        grid=(b // bb,),
        in_specs=[
            pl.BlockSpec((s, bb, d), lambda i: (0, i, 0)),
            pl.BlockSpec((CONV_BUF + s, bb, d), lambda i: (0, i, 0)),
            pl.BlockSpec((CONV_WIDTH, d), lambda i: (0, 0)),
            vec, vec, vec,
            pl.BlockSpec((d, d), lambda i: (0, 0)),
            vec,
        ],
        out_specs=pl.BlockSpec((s, bb, d), lambda i: (0, i, 0)),
        out_shape=jax.ShapeDtypeStruct((s, b, d), F32),
        compiler_params=_cparams("parallel"),
        name="conv_sample",
    )(x_t, ext_t, w_dw, row(b_dw), row(ln_g), row(ln_b), w_pw2, row(b_pw2))
    return jnp.transpose(out_t, (1, 0, 2)), ext[:, -CONV_BUF:, :]


def _lambda_value(lam_ref, lam_init):
    lp = lam_ref[...]
    l1 = jnp.sum(lp[0:1, :] * lp[1:2, :], axis=-1, keepdims=True)
    l2 = jnp.sum(lp[2:3, :] * lp[3:4, :], axis=-1, keepdims=True)
    return jnp.exp(l1) - jnp.exp(l2) + lam_init


def _flash_body(q_ref, k_ref, v_ref, lam_ref, sub_ref, o_ref, *, lam_init, tq):
    qi = pl.program_id(2)
    q = q_ref[...] * ATT_SCALE
    lane = lax.broadcasted_iota(jnp.int32, q.shape, 1)
    zero = jnp.zeros_like(q)
    qs = jnp.concatenate([jnp.where(lane < ATT_DH, q, zero), jnp.where(lane >= ATT_DH, q, zero)], axis=0)
    nt = (((1,), (1,)), ((), ()))

    def chunk(c, carry, masked):
        m, l, acc = carry
        kc = k_ref[pl.ds(pl.multiple_of(c * tq, tq), tq), :]
        vc = v_ref[pl.ds(pl.multiple_of(c * tq, tq), tq), :]
        s = lax.dot_general(qs, kc, nt, preferred_element_type=F32)
        if masked:
            row = lax.broadcasted_iota(jnp.int32, s.shape, 0) % tq
            col = lax.broadcasted_iota(jnp.int32, s.shape, 1)
            s = jnp.where(col <= row, s, NEG_INF)
        m_new = jnp.maximum(m, jnp.max(s, axis=-1, keepdims=True))
        alpha = jnp.exp(m - m_new)
        p = jnp.exp(s - m_new)
        l = l * alpha + jnp.sum(p, axis=-1, keepdims=True)
        acc = acc * alpha + jnp.dot(p.astype(BF16), vc, preferred_element_type=F32)
        return m_new, l, acc

    init = (jnp.full((2 * tq, 1), NEG_INF, F32), jnp.zeros((2 * tq, 1), F32), jnp.zeros((2 * tq, ATT_VD), F32))
    carry = lax.fori_loop(0, qi, lambda c, cr: chunk(c, cr, False), init)
    _, l, acc = chunk(qi, carry, True)
    lam = _lambda_value(lam_ref, lam_init)
    o = acc / l
    o = o[:tq] - lam * o[tq:]
    o_ref[...] = _rms(o, sub_ref[...], SUBLN_EPS) * (1.0 - lam_init)


def _flash_prompt(qkv, b, t, lam_p, subln, lam_init, tq):
    d = qkv.shape[-1]
    return pl.pallas_call(
        functools.partial(_flash_body, lam_init=lam_init, tq=tq),
        grid=(b, ATT_HEADS, t // tq),
        in_specs=[
            pl.BlockSpec((None, None, tq, ATT_VD), lambda i, h, j: (0, i, j, h)),
            pl.BlockSpec((None, None, t, ATT_VD), lambda i, h, j: (1, i, 0, h)),
            pl.BlockSpec((None, None, t, ATT_VD), lambda i, h, j: (2, i, 0, h)),
            pl.BlockSpec((4, ATT_DH), lambda i, h, j: (0, 0)),
            pl.BlockSpec((1, ATT_VD), lambda i, h, j: (0, 0)),
        ],
        out_specs=pl.BlockSpec((None, tq, ATT_VD), lambda i, h, j: (i, j, h)),
        out_shape=jax.ShapeDtypeStruct((b, t, d), F32),
        compiler_params=_cparams("parallel", "parallel", "arbitrary"),
        name="flash_prompt",
    )(qkv, qkv, qkv, lam_p, subln.reshape(1, ATT_VD))


def _paged_body(*refs, lam_init, pages_per_step):
    pp = pages_per_step
    pt_ref = refs[0]
    q_ref, kn_ref, vn_ref = refs[1:4]
    kp_refs = refs[4:4 + pp]
    vp_refs = refs[4 + pp:4 + 2 * pp]
    lam_ref, sub_ref, o_ref, qm_ref, m_ref, l_ref, acc_ref = refs[4 + 2 * pp:]
    del pt_ref
    step = pl.program_id(1)
    s_len = o_ref.shape[0]
    grp = q_ref.shape[0]
    nt = (((1,), (1,)), ((), ()))
    heads = range(ATT_HEADS)

    def head_cols(x, h):
        return x[:, h * ATT_VD:(h + 1) * ATT_VD]

    @pl.when(step == 0)
    def _():
        q = q_ref[...] * ATT_SCALE
        r = lax.broadcasted_iota(jnp.int32, q.shape, 0)
        c = lax.broadcasted_iota(jnp.int32, q.shape, 1)
        q = jnp.where((c % ATT_VD) // ATT_DH == r // s_len, q, 0.0).astype(BF16)
        qm_ref[...] = jnp.concatenate([head_cols(q, h) for h in heads], axis=0)
        m_ref[...] = jnp.full(m_ref.shape, NEG_INF, F32)
        l_ref[...] = jnp.zeros(l_ref.shape, F32)
        acc_ref[...] = jnp.zeros(acc_ref.shape, F32)

    qm = qm_ref[...]

    def head_rows(x, h):
        return x[h * grp:(h + 1) * grp]

    def merge(s, v_of_head):
        m = m_ref[...]
        m_new = jnp.maximum(m, jnp.max(s, axis=-1, keepdims=True))
        alpha = jnp.exp(m - m_new)
        p = jnp.exp(s - m_new)
        l_ref[...] = l_ref[...] * alpha + jnp.sum(p, axis=-1, keepdims=True)
        pb = p.astype(BF16)
        pv = jnp.concatenate([jnp.dot(head_rows(pb, h), v_of_head(h), preferred_element_type=F32)
                              for h in heads], axis=0)
        acc_ref[...] = acc_ref[...] * alpha + pv
        m_ref[...] = m_new

    def scores(k_of_head):
        return jnp.concatenate([lax.dot_general(head_rows(qm, h), k_of_head(h), nt, preferred_element_type=F32)
                                for h in heads], axis=0)

    for kp_ref, vp_ref in zip(kp_refs, vp_refs):
        s = scores(lambda h: kp_ref[:, h, :].astype(BF16))
        merge(s, lambda h: vp_ref[:, h, :].astype(BF16))

    @pl.when(step == pl.num_programs(1) - 1)
    def _():
        kn = kn_ref[...].astype(BF16)
        vn = vn_ref[...].astype(BF16)
        s = scores(lambda h: head_cols(kn, h))
        t = lax.broadcasted_iota(jnp.int32, s.shape, 0) % s_len
        c = lax.broadcasted_iota(jnp.int32, s.shape, 1)
        s = jnp.where(c <= t, s, NEG_INF)
        merge(s, lambda h: head_cols(vn, h))
        o = acc_ref[...] / l_ref[...]
        lam = _lambda_value(lam_ref, lam_init)
        for h in heads:
            blk = head_rows(o, h)
            oh = blk[:s_len] - lam * blk[s_len:]
            o_ref[:, h * ATT_VD:(h + 1) * ATT_VD] = _rms(oh, sub_ref[...], SUBLN_EPS) * (1.0 - lam_init)


def _paged_sample(q, k_new, v_new, cache_k, cache_v, layer, page_table, lam_p, subln, lam_init, pages_per_step):
    b, s, d = q.shape
    n_pages = page_table.shape[1]
    pp = pages_per_step
    grp = 2 * s
    rows = ATT_HEADS * grp
    q2 = jnp.concatenate([q, q], axis=1)
    k_new, v_new = (jnp.pad(z, ((0, 0), (0, grp - s), (0, 0))) for z in (k_new, v_new))
    tok = pl.BlockSpec((None, grp, d), lambda i, j, pt: (i, 0, 0))

    def page_spec(o):
        return pl.BlockSpec((None, None, PAGE_SIZE, ATT_HEADS, ATT_VD),
                            lambda i, j, pt: (layer, pt[i, j * pp + o], 0, 0, 0))

    grid_spec = pltpu.PrefetchScalarGridSpec(
        num_scalar_prefetch=1,
        grid=(b, n_pages // pp),
        in_specs=[tok, tok, tok] + [page_spec(o) for o in range(pp)] + [page_spec(o) for o in range(pp)] + [
            pl.BlockSpec((4, ATT_DH), lambda i, j, pt: (0, 0)),
            pl.BlockSpec((1, ATT_VD), lambda i, j, pt: (0, 0)),
        ],
        out_specs=pl.BlockSpec((None, s, d), lambda i, j, pt: (i, 0, 0)),
        scratch_shapes=[pltpu.VMEM((rows, ATT_VD), BF16), pltpu.VMEM((rows, 1), F32), pltpu.VMEM((rows, 1), F32),
                        pltpu.VMEM((rows, ATT_VD), F32)],
    )
    return pl.pallas_call(
        functools.partial(_paged_body, lam_init=lam_init, pages_per_step=pp),
        grid_spec=grid_spec,
        out_shape=jax.ShapeDtypeStruct((b, s, d), F32),
        compiler_params=_cparams("parallel", "arbitrary"),
        name="paged_sample",
    )(page_table, q2, k_new, v_new, *([cache_k] * pp), *([cache_v] * pp), lam_p, subln.reshape(1, ATT_VD))


def kernel(x_prompt, x_sample, state_wkv, state_shift, state_conv, cache_k, cache_v, page_table, norm_mix, norm_ffn, norm_final, rwkv_mu, rwkv_w_rkv, rwkv_w0, rwkv_w1, rwkv_w2, rwkv_a0, rwkv_a1, rwkv_a2, rwkv_v0, rwkv_v1, rwkv_v2, rwkv_g1, rwkv_g2, rwkv_k_k, rwkv_k_a, rwkv_r_k, rwkv_ln_g, rwkv_ln_b, rwkv_w_o, conv_w_pw1, conv_b_pw1, conv_w_dw, conv_b_dw, conv_ln_g, conv_ln_b, conv_w_pw2, conv_b_pw2, attn_w_qkv, attn_lambda, attn_subln, attn_w_o, ffn_w13, ffn_w2, moe_router, moe_w13, moe_w2):
    d = D_MODEL
    bf = lambda z: z.astype(BF16)
    w_rkv, w1, w2, a1, a2, v1, v2, g1, g2, w_o = map(
        bf, (rwkv_w_rkv, rwkv_w1, rwkv_w2, rwkv_a1, rwkv_a2, rwkv_v1, rwkv_v2, rwkv_g1, rwkv_g2, rwkv_w_o))
    c_pw1, c_pw2 = bf(conv_w_pw1), bf(conv_w_pw2)
    a_qkv, a_wo = bf(attn_w_qkv), bf(attn_w_o)
    f_w13, f_w2 = bf(ffn_w13), bf(ffn_w2)
    m_w13, m_w2 = bf(moe_w13), bf(moe_w2)
    d_ff = ffn_w2.shape[1]
    n_pool = cache_k.shape[1]

    def run(x3, wkv_in, shift_in, conv_in, sample):
        b, t, _ = x3.shape
        m = b * t
        x = x3.reshape(m, d)
        wkv_out, shift_out, conv_out, k_out, v_out = [], [], [], [], []
        v_first = None
        for i in range(DEPTH):
            kind, j = i % 3, i // 3
            if kind == 0:
                vres = None if j == 0 else (rwkv_v0[j - 1], v1[j - 1], v2[j - 1])
                x, sh, s_new, v_first = _rwkv_layer(
                    x, b, t, shift_in[j], wkv_in[j], v_first, vres, norm_mix[i], rwkv_mu[j], w_rkv[j],
                    rwkv_w0[j], w1[j], w2[j], rwkv_a0[j], a1[j], a2[j], g1[j], g2[j], rwkv_k_k[j],
                    rwkv_k_a[j], rwkv_r_k[j], rwkv_ln_g[j], rwkv_ln_b[j], w_o[j])
                wkv_out.append(s_new)
                shift_out.append(sh)
            elif kind == 1:
                conv_args = (norm_mix[i], c_pw1[j], conv_b_pw1[j], conv_w_dw[j], conv_b_dw[j], conv_ln_g[j],
                             conv_ln_b[j], c_pw2[j], conv_b_pw2[j])
                if sample:
                    y3, buf = _conv_sample(x.reshape(b, t, d), conv_in[j], *conv_args, 32)
                else:
                    y3, buf = _conv_prompt(x.reshape(b, t, d), conv_in[j], *conv_args, 256)
                x = y3.reshape(m, d)
                conv_out.append(buf)
            else:
                lam_init = 0.8 - 0.6 * math.exp(-0.3 * i)
                qkv, qkv_b = _matmul(x, a_qkv[j], norm_g=norm_mix[i], col_tile=d, split_out=True,
                                     bf16_copy=True)
                if sample:
                    q3, k3, v3 = (qkv[n].reshape(b, t, d) for n in range(3))
                    o = _paged_sample(q3, k3, v3, cache_k, cache_v, j, page_table, attn_lambda[j],
                                      attn_subln[j], lam_init, 4)
                else:
                    o = _flash_prompt(qkv_b.reshape(3, b, t, d), b, t, attn_lambda[j], attn_subln[j], lam_init, 256)
                x = _matmul(o.reshape(m, d), a_wo[j], res=x)
                k_out.append(qkv[1].reshape(b, t, ATT_HEADS, 2 * ATT_DH))
                v_out.append(qkv[2].reshape(b, t, ATT_HEADS, ATT_VD))
            if i % 2 == 0:
                x = _ffn_dense(x, norm_ffn[i], f_w13[i // 2], f_w2[i // 2], d_ff // 2)
            else:
                x = _moe(x, norm_ffn[i], moe_router[i // 2], m_w13[i // 2], m_w2[i // 2], 512)
        y = _rmsnorm(x, norm_final).reshape(b, t, d)
        return (y, jnp.stack(wkv_out), jnp.stack(shift_out), jnp.stack(conv_out), jnp.stack(k_out),
                jnp.stack(v_out))

    bp = x_prompt.shape[0]
    wkv0 = jnp.zeros((state_wkv.shape[0], bp) + state_wkv.shape[2:], state_wkv.dtype)
    shift0 = jnp.zeros((state_shift.shape[0], bp, d), x_prompt.dtype)
    conv0 = jnp.zeros((state_conv.shape[0], bp, CONV_BUF, d), x_prompt.dtype)
    y_p, wkv_p, sh_p, conv_p, k_p, v_p = run(x_prompt, wkv0, shift0, conv0, False)
    y_s, wkv_s, sh_s, conv_s, k_s, v_s = run(x_sample, state_wkv, state_shift, state_conv, True)
    return (y_p, y_s, wkv_p, wkv_s, sh_p, sh_s, conv_p, conv_s, k_p, k_s, v_p, v_s)
```

```python
import functools
import math

import jax
import jax.numpy as jnp
from jax import lax
from jax.experimental import pallas as pl
from jax.experimental.pallas import tpu as pltpu

F32 = jnp.float32
BF16 = jnp.bfloat16

D_MODEL = 1024
DEPTH = 4
PAGE_SIZE = 128
RWKV_HEAD = 64
RWKV_HEADS = D_MODEL // RWKV_HEAD
GN_EPS = 64e-5
CONV_WIDTH = 31
CONV_BUF = CONV_WIDTH - 1
LN_EPS = 1e-5
ATT_HEADS = 8
ATT_DH = 64
ATT_VD = 128
ATT_SCALE = ATT_DH ** -0.5
SUBLN_EPS = 1e-5
N_EXPERTS = 8
TOP_K = 2
RMS_EPS = 1e-6

V7X_LANES = 128
V7X_SUBLANES = 8
V7X_VMEM_LIMIT_BYTES = 56 * 1024 * 1024

ROW_TILE = 512
EXPERT_TILE_MIN = 128
EXPERT_TILE_MAX = 1024
CONV_HALO = 32
NEG_INF = float("-inf")


def _cparams(*sem):
    return pltpu.CompilerParams(dimension_semantics=sem, vmem_limit_bytes=V7X_VMEM_LIMIT_BYTES)


def _rms(x, g, eps=RMS_EPS):
    return x * lax.rsqrt(jnp.mean(x * x, axis=-1, keepdims=True) + eps) * g


def _sigmoid(x):
    return 1.0 / (1.0 + jnp.exp(-x))


def _silu(x):
    return x * _sigmoid(x)


def _row_tile(m):
    return min(ROW_TILE, m)


def _rms_body(x_ref, g_ref, o_ref):
    o_ref[...] = _rms(x_ref[...], g_ref[...]).astype(o_ref.dtype)


def _rmsnorm(x, g, out_dtype=F32):
    m, d = x.shape
    tm = _row_tile(m)
    return pl.pallas_call(
        _rms_body,
        grid=(m // tm,),
        in_specs=[pl.BlockSpec((tm, d), lambda i: (i, 0)), pl.BlockSpec((1, d), lambda i: (0, 0))],
        out_specs=pl.BlockSpec((tm, d), lambda i: (i, 0)),
        out_shape=jax.ShapeDtypeStruct((m, d), out_dtype),
        compiler_params=_cparams("parallel"),
        name="rmsnorm",
    )(x, g.reshape(1, d))


def _mm_body(*refs, norm, mul, bias, act, res, glu, bf16_copy):
    it = iter(refs)
    x_ref = next(it)
    g_ref = next(it) if norm else None
    m_ref = next(it) if mul else None
    w_ref = next(it)
    w2_ref = next(it) if glu else None
    b_ref = next(it) if bias else None
    b2_ref = next(it) if (bias and glu) else None
    r_ref = next(it) if res else None
    o_ref = next(it)
    ob_ref = next(it) if bf16_copy else None
    xs_ref = next(it)

    @pl.when(pl.program_id(1) == 0)
    def _():
        x = x_ref[...].astype(F32)
        if norm:
            x = _rms(x, g_ref[...])
        if mul:
            x = x * m_ref[...]
        xs_ref[...] = x.astype(BF16)

    xs = xs_ref[...]
    y = jnp.dot(xs, w_ref[...], preferred_element_type=F32)
    if bias:
        y = y + b_ref[...]
    if glu:
        y2 = jnp.dot(xs, w2_ref[...], preferred_element_type=F32)
        if bias:
            y2 = y2 + b2_ref[...]
        y = y * _sigmoid(y2)
    if act == "tanh":
        y = jnp.tanh(y)
    elif act == "sigmoid":
        y = _sigmoid(y)
    if res:
        y = y + r_ref[...]
    o_ref[...] = y.astype(o_ref.dtype)
    if bf16_copy:
        ob_ref[...] = y.astype(BF16)


def _matmul(x, w, *, norm_g=None, mul=None, bias=None, act=None, res=None, glu=False,
            col_tile=None, split_out=False, bf16_copy=False):
    m, k = x.shape
    n = w.shape[1]
    n_out = n // 2 if glu else n
    tm = _row_tile(m)
    tn = col_tile or n_out
    nj = n_out // tn
    in_specs = [pl.BlockSpec((tm, k), lambda i, j: (i, 0))]
    args = [x]
    if norm_g is not None:
        in_specs.append(pl.BlockSpec((1, k), lambda i, j: (0, 0)))
        args.append(norm_g.reshape(1, k))
    if mul is not None:
        in_specs.append(pl.BlockSpec((tm, k), lambda i, j: (i, 0)))
        args.append(mul)
    in_specs.append(pl.BlockSpec((k, tn), lambda i, j: (0, j)))
    args.append(w)
    if glu:
        in_specs.append(pl.BlockSpec((k, tn), lambda i, j: (0, j + nj)))
        args.append(w)
    if bias is not None:
        b2d = bias.reshape(1, n)
        in_specs.append(pl.BlockSpec((1, tn), lambda i, j: (0, j)))
        args.append(b2d)
        if glu:
            in_specs.append(pl.BlockSpec((1, tn), lambda i, j: (0, j + nj)))
            args.append(b2d)
    if res is not None:
        in_specs.append(pl.BlockSpec((tm, tn), lambda i, j: (i, j)))
        args.append(res)
    if split_out:
        out_shape = jax.ShapeDtypeStruct((nj, m, tn), F32)
        out_spec = pl.BlockSpec((None, tm, tn), lambda i, j: (j, i, 0))
    else:
        out_shape = jax.ShapeDtypeStruct((m, n_out), F32)
        out_spec = pl.BlockSpec((tm, tn), lambda i, j: (i, j))
    if bf16_copy:
        out_shape = [out_shape, jax.ShapeDtypeStruct(out_shape.shape, BF16)]
        out_spec = [out_spec, out_spec]
    body = functools.partial(_mm_body, norm=norm_g is not None, mul=mul is not None,
                             bias=bias is not None, act=act, res=res is not None, glu=glu,
                             bf16_copy=bf16_copy)
    return pl.pallas_call(
        body,
        grid=(m // tm, nj),
        in_specs=in_specs,
        out_specs=out_spec,
        out_shape=out_shape,
        scratch_shapes=[pltpu.VMEM((tm, k), BF16)],
        compiler_params=_cparams("parallel", "arbitrary"),
        name="matmul",
    )(*args)


def _rows_from_tiles(ref):
    return jnp.concatenate([ref[:, s, :] for s in range(ref.shape[1])], axis=-1)


def _rows_to_tiles(ref, val):
    for s in range(ref.shape[1]):
        ref[:, s, :] = val[:, s * V7X_LANES:(s + 1) * V7X_LANES]


def _ffn_body(*refs, moe):
    if moe:
        te_ref, tv_ref, x_ref, g_ref, wg_ref, wu_ref, wd_ref, o_ref, hs_ref, acc_ref = refs
    else:
        x_ref, g_ref, wg_ref, wu_ref, wd_ref, o_ref, hs_ref, acc_ref = refs
    j = pl.program_id(1)
    nj = pl.num_programs(1)

    def compute():
        @pl.when(j == 0)
        def _():
            x = _rows_from_tiles(x_ref) if moe else x_ref[...]
            hs_ref[...] = _rms(x, g_ref[...]).astype(BF16)
            acc_ref[...] = jnp.zeros_like(acc_ref)

        h = hs_ref[...]
        gate = jnp.dot(h, wg_ref[...], preferred_element_type=F32)
        up = jnp.dot(h, wu_ref[...], preferred_element_type=F32)
        a = (_silu(gate) * up).astype(BF16)
        acc_ref[...] += jnp.dot(a, wd_ref[...], preferred_element_type=F32)

        @pl.when(j == nj - 1)
        def _():
            if moe:
                _rows_to_tiles(o_ref, acc_ref[...])
            else:
                o_ref[...] = x_ref[...] + acc_ref[...]

    if moe:
        valid = tv_ref[pl.program_id(0)] > 0
        pl.when(valid)(compute)

        @pl.when(jnp.logical_and(jnp.logical_not(valid), j == nj - 1))
        def _():
            o_ref[...] = jnp.zeros_like(o_ref)
    else:
        compute()


def _ffn_dense(x, g, w13, w2, ff_tile):
    m, d = x.shape
    f = w2.shape[0]
    tm = _row_tile(m)
    nf = f // ff_tile
    return pl.pallas_call(
        functools.partial(_ffn_body, moe=False),
        grid=(m // tm, nf),
        in_specs=[
            pl.BlockSpec((tm, d), lambda i, j: (i, 0)),
            pl.BlockSpec((1, d), lambda i, j: (0, 0)),
            pl.BlockSpec((d, ff_tile), lambda i, j: (0, j)),
            pl.BlockSpec((d, ff_tile), lambda i, j: (0, j + nf)),
            pl.BlockSpec((ff_tile, d), lambda i, j: (j, 0)),
        ],
        out_specs=pl.BlockSpec((tm, d), lambda i, j: (i, 0)),
        out_shape=jax.ShapeDtypeStruct((m, d), F32),
        scratch_shapes=[pltpu.VMEM((tm, d), BF16), pltpu.VMEM((tm, d), F32)],
        compiler_params=_cparams("parallel", "arbitrary"),
        name="ffn_dense",
    )(x, g.reshape(1, d), w13, w13, w2)


def _ffn_experts(xs, g, w13, w2, tile_expert, tile_valid, tm, ff_tile):
    s, sub, lanes = xs.shape
    d = sub * lanes
    f = w2.shape[1]
    nf = f // ff_tile
    tok = pl.BlockSpec((tm, sub, lanes), lambda i, j, te, tv: (i, 0, 0))
    grid_spec = pltpu.PrefetchScalarGridSpec(
        num_scalar_prefetch=2,
        grid=(s // tm, nf),
        in_specs=[
            tok,
            pl.BlockSpec((1, d), lambda i, j, te, tv: (0, 0)),
            pl.BlockSpec((None, d, ff_tile), lambda i, j, te, tv: (te[i], 0, j)),
            pl.BlockSpec((None, d, ff_tile), lambda i, j, te, tv: (te[i], 0, j + nf)),
            pl.BlockSpec((None, ff_tile, d), lambda i, j, te, tv: (te[i], j, 0)),
        ],
        out_specs=tok,
        scratch_shapes=[pltpu.VMEM((tm, d), BF16), pltpu.VMEM((tm, d), F32)],
    )
    return pl.pallas_call(
        functools.partial(_ffn_body, moe=True),
        grid_spec=grid_spec,
        out_shape=jax.ShapeDtypeStruct((s, sub, lanes), F32),
        compiler_params=_cparams("parallel", "arbitrary"),
        name="ffn_experts",
    )(tile_expert, tile_valid, xs, g.reshape(1, d), w13, w13, w2)


def _router_body(x_ref, g_ref, whi_ref, wlo_ref, o_ref, g1_ref, g2_ref):
    h = _rms(x_ref[...], g_ref[...])
    hhi = h.astype(BF16)
    hlo = (h - hhi.astype(F32)).astype(BF16)
    whi = whi_ref[...]
    lg = (jnp.dot(hhi, whi, preferred_element_type=F32)
          + jnp.dot(hlo, whi, preferred_element_type=F32)
          + jnp.dot(hhi, wlo_ref[...], preferred_element_type=F32))
    lane = lax.broadcasted_iota(jnp.int32, lg.shape, 1)
    lg = jnp.where(lane < N_EXPERTS, lg, NEG_INF)
    m1 = jnp.max(lg, axis=-1, keepdims=True)
    i1 = jnp.min(jnp.where(lg == m1, lane, V7X_LANES), axis=-1, keepdims=True)
    lg2 = jnp.where(lane == i1, NEG_INF, lg)
    m2 = jnp.max(lg2, axis=-1, keepdims=True)
    i2 = jnp.min(jnp.where(lg2 == m2, lane, V7X_LANES), axis=-1, keepdims=True)
    e2 = jnp.exp(m2 - m1)
    g1 = 1.0 / (1.0 + e2)
    g2 = e2 / (1.0 + e2)
    out = jnp.where(lane == 0, i1.astype(F32),
                    jnp.where(lane == 1, i2.astype(F32),
                              jnp.where(lane == 2, g1, jnp.where(lane == 3, g2, 0.0))))
    o_ref[...] = out
    g1_ref[:, 0, :] = jnp.broadcast_to(g1, lg.shape)
    g2_ref[:, 0, :] = jnp.broadcast_to(g2, lg.shape)


def _router(x, g, w_router):
    m, d = x.shape
    tm = _row_tile(m)
    wpad = jnp.zeros((d, V7X_LANES), F32).at[:, :N_EXPERTS].set(w_router)
    whi = wpad.astype(BF16)
    wlo = (wpad - whi.astype(F32)).astype(BF16)
    gate_spec = pl.BlockSpec((tm, 1, V7X_LANES), lambda i: (i, 0, 0))
    gate_shape = jax.ShapeDtypeStruct((m, 1, V7X_LANES), F32)
    return pl.pallas_call(
        _router_body,
        grid=(m // tm,),
        in_specs=[
            pl.BlockSpec((tm, d), lambda i: (i, 0)),
            pl.BlockSpec((1, d), lambda i: (0, 0)),
            pl.BlockSpec((d, V7X_LANES), lambda i: (0, 0)),
            pl.BlockSpec((d, V7X_LANES), lambda i: (0, 0)),
        ],
        out_specs=[pl.BlockSpec((tm, V7X_LANES), lambda i: (i, 0)), gate_spec, gate_spec],
        out_shape=[jax.ShapeDtypeStruct((m, V7X_LANES), F32), gate_shape, gate_shape],
        compiler_params=_cparams("parallel"),
        name="router",
    )(x, g.reshape(1, d), whi, wlo)


DMA_UNROLL = 8


def _gather_body(src_ref, x_hbm, o_ref, sem):
    tg = o_ref.shape[0]
    base = pl.program_id(0) * tg

    def tile_copy(r):
        return pltpu.make_async_copy(x_hbm.at[pl.ds(src_ref[base + r], 1)], o_ref.at[pl.ds(r, 1)], sem)

    def start(i, c):
        for u in range(DMA_UNROLL):
            tile_copy(i * DMA_UNROLL + u).start(priority=u % 2)
        return c

    def wait(i, c):
        for u in range(DMA_UNROLL):
            tile_copy(i * DMA_UNROLL + u).wait()
        return c

    lax.fori_loop(0, tg // DMA_UNROLL, start, 0)
    lax.fori_loop(0, tg // DMA_UNROLL, wait, 0)


def _gather_tokens(x3, src, tg):
    s = src.shape[0]
    _, sub, lanes = x3.shape
    grid_spec = pltpu.PrefetchScalarGridSpec(
        num_scalar_prefetch=1,
        grid=(s // tg,),
        in_specs=[pl.BlockSpec(memory_space=pl.ANY)],
        out_specs=pl.BlockSpec((tg, sub, lanes), lambda i, src: (i, 0, 0)),
        scratch_shapes=[pltpu.SemaphoreType.DMA(())],
    )
    return pl.pallas_call(
        _gather_body,
        grid_spec=grid_spec,
        out_shape=jax.ShapeDtypeStruct((s, sub, lanes), x3.dtype),
        compiler_params=_cparams("arbitrary"),
        name="moe_gather",
    )(src, x3)


def _combine_body(dest_ref, x_ref, g1_ref, g2_ref, y_hbm, o_ref, buf, sem):
    tc = o_ref.shape[0]
    base = pl.program_id(0) * tc

    def tile_copy(r, k):
        return pltpu.make_async_copy(y_hbm.at[pl.ds(dest_ref[TOP_K * (base + r) + k], 1)],
                                     buf.at[k, pl.ds(r, 1)], sem)

    def start(i, c):
        for u in range(DMA_UNROLL):
            for k in range(TOP_K):
                tile_copy(i * DMA_UNROLL + u, k).start(priority=k)
        return c

    def wait(i, c):
        for u in range(DMA_UNROLL):
            for k in range(TOP_K):
                tile_copy(i * DMA_UNROLL + u, k).wait()
        return c

    lax.fori_loop(0, tc // DMA_UNROLL, start, 0)
    lax.fori_loop(0, tc // DMA_UNROLL, wait, 0)
    o_ref[...] = x_ref[...] + g1_ref[...] * buf[0] + g2_ref[...] * buf[1]


def _moe_combine(x3, g1, g2, y3, dest, tc):
    m, sub, lanes = x3.shape
    tok = pl.BlockSpec((tc, sub, lanes), lambda i, dest: (i, 0, 0))
    gate = pl.BlockSpec((tc, 1, lanes), lambda i, dest: (i, 0, 0))
    grid_spec = pltpu.PrefetchScalarGridSpec(
        num_scalar_prefetch=1,
        grid=(m // tc,),
        in_specs=[tok, gate, gate, pl.BlockSpec(memory_space=pl.ANY)],
        out_specs=tok,
        scratch_shapes=[pltpu.VMEM((TOP_K, tc, sub, lanes), F32), pltpu.SemaphoreType.DMA(())],
    )
    return pl.pallas_call(
        _combine_body,
        grid_spec=grid_spec,
        out_shape=jax.ShapeDtypeStruct((m, sub, lanes), F32),
        compiler_params=_cparams("arbitrary"),
        name="moe_combine",
    )(dest, x3, g1, g2, y3)


def _moe(x, g, w_router, w13, w2, ff_tile):
    m, d = x.shape
    tm = max(EXPERT_TILE_MIN, min(EXPERT_TILE_MAX, m // 16))
    r, g1, g2 = _router(x, g, w_router)
    e_flat = r[:, :TOP_K].astype(jnp.int32).reshape(-1)
    n_slots = m * TOP_K
    n_tiles = n_slots // tm + N_EXPERTS
    s_pad = n_tiles * tm
    onehot = (e_flat[:, None] == jnp.arange(N_EXPERTS, dtype=jnp.int32)[None, :]).astype(jnp.int32)
    rank = jnp.sum((jnp.cumsum(onehot, axis=0) - 1) * onehot, axis=1)
    counts = jnp.sum(onehot, axis=0)
    padded = ((counts + tm - 1) // tm) * tm
    ends = jnp.cumsum(padded)
    starts = ends - padded
    dest = (jnp.sum(onehot * starts[None, :], axis=1) + rank).astype(jnp.int32)
    token = jnp.arange(n_slots, dtype=jnp.int32) // TOP_K
    src = (jnp.arange(s_pad, dtype=jnp.int32) % m).at[dest].set(token)
    tile_start = jnp.arange(n_tiles, dtype=jnp.int32) * tm
    tile_valid = (tile_start < ends[-1]).astype(jnp.int32)
    tile_expert = jnp.sum((tile_start[:, None] >= ends[None, :]).astype(jnp.int32), axis=1)
    last_expert = jnp.max(jnp.where(tile_valid > 0, tile_expert, 0))
    tile_expert = jnp.where(tile_valid > 0, tile_expert, last_expert).astype(jnp.int32)
    x3 = x.reshape(m, d // V7X_LANES, V7X_LANES)
    xs = _gather_tokens(x3, src, tm)
    ys = _ffn_experts(xs, g, w13, w2, tile_expert, tile_valid, tm, ff_tile)
    return _moe_combine(x3, g1, g2, ys, dest, min(256, m)).reshape(m, d)


def _wkv_body(r_ref, wl_ref, k_ref, v_ref, a_ref, kk_ref, ka_ref, rk_ref, lg_ref, lb_ref, s0_ref,
              y_ref, s_ref, w_s, a_s, b_s, k_s, r_s):
    tc = pl.program_id(1)
    n = RWKV_HEAD
    tt = r_ref.shape[0]

    @pl.when(tc == 0)
    def _():
        s_ref[...] = s0_ref[...]

    kkw = kk_ref[...]
    kaw = ka_ref[...]
    rkw = rk_ref[...]
    lgw = lg_ref[...]
    lbw = lb_ref[...]

    def unit_key(t):
        kk = k_ref[t] * kkw
        return kk / jnp.maximum(jnp.sqrt(jnp.sum(kk * kk, axis=0, keepdims=True)), 1e-12)

    def bc(ref, i):
        return jnp.broadcast_to(ref[pl.ds(i, 1), :], (n, V7X_LANES))

    a_s[...] = -unit_key(0)
    sa0 = jnp.zeros((n, V7X_LANES), F32)
    for i in range(n):
        sa0 = sa0 + s_ref[i] * bc(a_s, i)

    def step(t, sa):
        r = r_ref[t]
        k = k_ref[t]
        v = v_ref[t]
        a = a_ref[t]
        k2 = k * (1.0 + (a - 1.0) * kaw)
        w_s[...] = jnp.exp(-jnp.exp(wl_ref[t]))
        b_s[...] = unit_key(t) * a
        k_s[...] = k2
        r_s[...] = r
        a_s[...] = -unit_key(jnp.minimum(t + 1, tt - 1))
        y = jnp.zeros((n, V7X_LANES), F32)
        sa_next = jnp.zeros((n, V7X_LANES), F32)
        for i in range(n):
            s_new = s_ref[i] * bc(w_s, i) + sa * bc(b_s, i) + v * bc(k_s, i)
            s_ref[i] = s_new
            y = y + s_new * bc(r_s, i)
            sa_next = sa_next + s_new * bc(a_s, i)
        ym = jnp.mean(y, axis=0, keepdims=True)
        yc = y - ym
        yn = yc * lax.rsqrt(jnp.mean(yc * yc, axis=0, keepdims=True) + GN_EPS)
        bonus = jnp.sum(r * k2 * rkw, axis=0, keepdims=True) * v
        y_ref[t] = yn * lgw + lbw + bonus
        return sa_next

    lax.fori_loop(0, tt, step, sa0)


def _wkv(r, wl, k, v, a, k_k, k_a, r_k, ln_g, ln_b, s0, time_tile):
    t, n, l = r.shape
    tt = min(time_tile, t)
    seq = pl.BlockSpec((tt, n, V7X_LANES), lambda i, j: (j, 0, i))
    par = pl.BlockSpec((n, V7X_LANES), lambda i, j: (0, i))
    st = pl.BlockSpec((n, n, V7X_LANES), lambda i, j: (0, 0, i))
    return pl.pallas_call(
        _wkv_body,
        grid=(l // V7X_LANES, t // tt),
        in_specs=[seq, seq, seq, seq, seq, par, par, par, par, par, st],
        out_specs=[seq, st],
        out_shape=[jax.ShapeDtypeStruct((t, n, l), F32), jax.ShapeDtypeStruct((n, n, l), F32)],
        scratch_shapes=[pltpu.VMEM((n, V7X_LANES), F32)] * 5,
        compiler_params=_cparams("parallel", "arbitrary"),
        name="wkv7",
    )(r, wl, k, v, a, k_k, k_a, r_k, ln_g, ln_b, s0)


def _softplus(z):
    return jnp.maximum(z, 0.0) + jnp.log1p(jnp.exp(-jnp.abs(z)))


def _rwkv_proj_body(*refs, has_vres):
    (h_ref, hp_ref, mu_ref, wr_ref, wk_ref, wv_ref, w0_ref, w1_ref, w2_ref,
     a0_ref, a1_ref, a2_ref, g1_ref, g2_ref) = refs[:14]
    n_in = 14
    if has_vres:
        vf_ref, v0_ref, v1_ref, v2_ref = refs[14:18]
        n_in = 18
    r_ref, k_ref, v_ref, wl_ref, a_ref, g_ref = refs[n_in:]
    h = h_ref[...]
    xx = hp_ref[...] - h

    def mix(i):
        return (h + xx * mu_ref[pl.ds(i, 1), :]).astype(BF16)

    def dot(x, w_ref):
        return jnp.dot(x.astype(BF16), w_ref[...], preferred_element_type=F32)

    r_ref[...] = dot(mix(0), wr_ref)
    z = w0_ref[...] + dot(jnp.tanh(dot(mix(1), w1_ref)), w2_ref)
    wl_ref[...] = -_softplus(-z) - 0.5
    k_ref[...] = dot(mix(2), wk_ref)
    xv = mix(3)
    v = dot(xv, wv_ref)
    if has_vres:
        v = v + (vf_ref[...] - v) * _sigmoid(v0_ref[...] + dot(dot(xv, v1_ref), v2_ref))
    v_ref[...] = v
    a_ref[...] = _sigmoid(a0_ref[...] + dot(dot(mix(4), a1_ref), a2_ref))
    g_ref[...] = dot(_sigmoid(dot(mix(5), g1_ref)), g2_ref)


def _rwkv_proj(h, h_prev, mu, w_rkv, w0, w1, w2, a0, a1, a2, g1, g2, v_first, vres):
    m, c = h.shape
    tm = min(256, m)
    tok = pl.BlockSpec((tm, c), lambda i: (i, 0))

    def whole(z):
        return pl.BlockSpec(z.shape, lambda i: (0,) * z.ndim)

    row = lambda z: z.reshape(1, c)
    consts = [mu, w_rkv[0], w_rkv[1], w_rkv[2], row(w0), w1, w2, row(a0), a1, a2, g1, g2]
    args = [h, h_prev] + consts
    in_specs = [tok, tok] + [whole(z) for z in consts]
    if vres is not None:
        v0, v1, v2 = vres
        extra = [row(v0), v1, v2]
        args += [v_first] + extra
        in_specs += [tok] + [whole(z) for z in extra]
    return pl.pallas_call(
        functools.partial(_rwkv_proj_body, has_vres=vres is not None),
        grid=(m // tm,),
        in_specs=in_specs,
        out_specs=[tok] * 6,
        out_shape=[jax.ShapeDtypeStruct((m, c), F32)] * 6,
        compiler_params=_cparams("parallel"),
        name="rwkv_proj",
    )(*args)


def _to_lanes(z, b, t):
    return jnp.transpose(z.reshape(b, t, RWKV_HEADS, RWKV_HEAD), (1, 3, 0, 2)).reshape(t, RWKV_HEAD, b * RWKV_HEADS)


def _head_param(p, b):
    ph = p.reshape(RWKV_HEADS, RWKV_HEAD).T
    return jnp.tile(ph, (1, b))


def _rwkv_layer(x, b, t, shift_prev, s0, v_first, vres, norm_g, mu, w_rkv, w0, w1, w2, a0, a1, a2,
                g1, g2, k_k, k_a, r_k, ln_g, ln_b, w_o):
    m, c = x.shape
    h = _rmsnorm(x, norm_g)
    h3 = h.reshape(b, t, c)
    h_prev = jnp.concatenate([shift_prev[:, None, :], h3[:, :-1]], axis=1)
    r, k, v, wl, a, g = _rwkv_proj(h, h_prev.reshape(m, c), mu, w_rkv, w0, w1, w2, a0, a1, a2, g1, g2,
                                   v_first, vres)
    if vres is None:
        v_first = v
    tl = lambda z: _to_lanes(z, b, t)
    hp = lambda p: _head_param(p, b)
    s0_l = jnp.transpose(s0, (3, 2, 0, 1)).reshape(RWKV_HEAD, RWKV_HEAD, b * RWKV_HEADS)
    y_l, s_l = _wkv(tl(r), tl(wl), tl(k), tl(v), tl(a), hp(k_k), hp(k_a), hp(r_k), hp(ln_g), hp(ln_b),
                    s0_l, 32)
    y = jnp.transpose(y_l.reshape(t, RWKV_HEAD, b, RWKV_HEADS), (2, 0, 3, 1)).reshape(m, c)
    s_new = jnp.transpose(s_l.reshape(RWKV_HEAD, RWKV_HEAD, b, RWKV_HEADS), (2, 3, 1, 0))
    out = _matmul(y, w_o, mul=g, res=x)
    return out, h3[:, -1], s_new, v_first


def _conv_tail(c, ln_g, ln_b):
    mu = jnp.mean(c, axis=-1, keepdims=True)
    xc = c - mu
    var = jnp.mean(xc * xc, axis=-1, keepdims=True)
    return _silu(xc * lax.rsqrt(var + LN_EPS) * ln_g + ln_b)


def _conv_prompt_body(x_ref, buf_ref, g_ref, wa_ref, wb_ref, ba_ref, bb_ref, wdw_ref, bdw_ref,
                      lg_ref, lb_ref, w2_ref, b2_ref, o_ref, tail_ref, ext_ref):
    ti = pl.program_id(1)
    tm = x_ref.shape[0]

    @pl.when(ti == 0)
    def _():
        ext_ref[pl.ds(0, CONV_HALO), :] = buf_ref[...]

    x = x_ref[...]
    h = _rms(x, g_ref[...]).astype(BF16)
    ua = jnp.dot(h, wa_ref[...], preferred_element_type=F32) + ba_ref[...]
    ub = jnp.dot(h, wb_ref[...], preferred_element_type=F32) + bb_ref[...]
    ext_ref[pl.ds(CONV_HALO, tm), :] = ua * _sigmoid(ub)
    c = jnp.zeros_like(x) + bdw_ref[...]
    for j in range(CONV_WIDTH):
        c = c + ext_ref[pl.ds(CONV_HALO - CONV_BUF + j, tm), :] * wdw_ref[pl.ds(j, 1), :]
    z = _conv_tail(c, lg_ref[...], lb_ref[...]).astype(BF16)
    o_ref[...] = x + jnp.dot(z, w2_ref[...], preferred_element_type=F32) + b2_ref[...]
    tail = ext_ref[pl.ds(tm, CONV_HALO), :]
    ext_ref[pl.ds(0, CONV_HALO), :] = tail

    @pl.when(ti == pl.num_programs(1) - 1)
    def _():
        tail_ref[...] = tail


def _conv_prompt(x, buf, norm_g, w_pw1, b_pw1, w_dw, b_dw, ln_g, ln_b, w_pw2, b_pw2, tm):
    b, t, d = x.shape
    bufp = jnp.pad(buf, ((0, 0), (CONV_HALO - CONV_BUF, 0), (0, 0)))
    row = lambda z: z.reshape(1, -1)
    vec = pl.BlockSpec((1, d), lambda i, j: (0, 0))
    out, tail = pl.pallas_call(
        _conv_prompt_body,
        grid=(b, t // tm),
        in_specs=[
            pl.BlockSpec((None, tm, d), lambda i, j: (i, j, 0)),
            pl.BlockSpec((None, CONV_HALO, d), lambda i, j: (i, 0, 0)),
            vec,
            pl.BlockSpec((d, d), lambda i, j: (0, 0)),
            pl.BlockSpec((d, d), lambda i, j: (0, 1)),
            pl.BlockSpec((1, d), lambda i, j: (0, 0)),
            pl.BlockSpec((1, d), lambda i, j: (0, 1)),
            pl.BlockSpec((CONV_WIDTH, d), lambda i, j: (0, 0)),
            vec, vec, vec,
            pl.BlockSpec((d, d), lambda i, j: (0, 0)),
            vec,
        ],
        out_specs=[pl.BlockSpec((None, tm, d), lambda i, j: (i, j, 0)),
                   pl.BlockSpec((None, CONV_HALO, d), lambda i, j: (i, 0, 0))],
        out_shape=[jax.ShapeDtypeStruct((b, t, d), F32), jax.ShapeDtypeStruct((b, CONV_HALO, d), F32)],
        scratch_shapes=[pltpu.VMEM((CONV_HALO + tm, d), F32)],
        compiler_params=_cparams("parallel", "arbitrary"),
        name="conv_prompt",
    )(x, bufp, row(norm_g), w_pw1, w_pw1, row(b_pw1), row(b_pw1), w_dw, row(b_dw), row(ln_g), row(ln_b),
      w_pw2, row(b_pw2))
    return out, tail[:, CONV_HALO - CONV_BUF:, :]


def _conv_sample_body(x_ref, ext_ref, wdw_ref, bdw_ref, lg_ref, lb_ref, w2_ref, b2_ref, o_ref):
    s = x_ref.shape[0]
    for t in range(s):
        c = jnp.zeros(x_ref.shape[1:], F32) + bdw_ref[...]
        for j in range(CONV_WIDTH):
            c = c + ext_ref[t + j] * wdw_ref[pl.ds(j, 1), :]
        z = _conv_tail(c, lg_ref[...], lb_ref[...]).astype(BF16)
        o_ref[t] = x_ref[t] + jnp.dot(z, w2_ref[...], preferred_element_type=F32) + b2_ref[...]


def _conv_sample(x, buf, norm_g, w_pw1, b_pw1, w_dw, b_dw, ln_g, ln_b, w_pw2, b_pw2, bb):
    b, s, d = x.shape
    u = _matmul(x.reshape(b * s, d), w_pw1, norm_g=norm_g, bias=b_pw1, glu=True).reshape(b, s, d)
    ext = jnp.concatenate([buf, u], axis=1)
    ext_t = jnp.transpose(ext, (1, 0, 2))
    x_t = jnp.transpose(x, (1, 0, 2))
    row = lambda z: z.reshape(1, -1)
    vec = pl.BlockSpec((1, d), lambda i: (0, 0))
    out_t = pl.pallas_call(
        _conv_sample_body,
        grid=(b // bb,),
        in_specs=[
            pl.BlockSpec((s, bb, d), lambda i: (0, i, 0)),
            pl.BlockSpec((CONV_BUF + s, bb, d), lambda i: (0, i, 0)),
            pl.BlockSpec((CONV_WIDTH, d), lambda i: (0, 0)),
            vec, vec, vec,
            pl.BlockSpec((d, d), lambda i: (0, 0)),
            vec,
        ],
        out_specs=pl.BlockSpec((s, bb, d), lambda i: (0, i, 0)),
        out_shape=jax.ShapeDtypeStruct((s, b, d), F32),
        compiler_params=_cparams("parallel"),
        name="conv_sample",
    )(x_t, ext_t, w_dw, row(b_dw), row(ln_g), row(ln_b), w_pw2, row(b_pw2))
    return jnp.transpose(out_t, (1, 0, 2)), ext[:, -CONV_BUF:, :]


def _lambda_value(lam_ref, lam_init):
    lp = lam_ref[...]
    l1 = jnp.sum(lp[0:1, :] * lp[1:2, :], axis=-1, keepdims=True)
    l2 = jnp.sum(lp[2:3, :] * lp[3:4, :], axis=-1, keepdims=True)
    return jnp.exp(l1) - jnp.exp(l2) + lam_init


def _flash_body(q_ref, k_ref, v_ref, lam_ref, sub_ref, o_ref, *, lam_init, tq):
    qi = pl.program_id(2)
    q = q_ref[...] * ATT_SCALE
    lane = lax.broadcasted_iota(jnp.int32, q.shape, 1)
    zero = jnp.zeros_like(q)
    qs = jnp.concatenate([jnp.where(lane < ATT_DH, q, zero), jnp.where(lane >= ATT_DH, q, zero)], axis=0)
    nt = (((1,), (1,)), ((), ()))

    def chunk(c, carry, masked):
        m, l, acc = carry
        kc = k_ref[pl.ds(pl.multiple_of(c * tq, tq), tq), :]
        vc = v_ref[pl.ds(pl.multiple_of(c * tq, tq), tq), :]
        s = lax.dot_general(qs, kc, nt, preferred_element_type=F32)
        if masked:
            row = lax.broadcasted_iota(jnp.int32, s.shape, 0) % tq
            col = lax.broadcasted_iota(jnp.int32, s.shape, 1)
            s = jnp.where(col <= row, s, NEG_INF)
        m_new = jnp.maximum(m, jnp.max(s, axis=-1, keepdims=True))
        alpha = jnp.exp(m - m_new)
        p = jnp.exp(s - m_new)
        l = l * alpha + jnp.sum(p, axis=-1, keepdims=True)
        acc = acc * alpha + jnp.dot(p.astype(BF16), vc, preferred_element_type=F32)
        return m_new, l, acc

    init = (jnp.full((2 * tq, 1), NEG_INF, F32), jnp.zeros((2 * tq, 1), F32), jnp.zeros((2 * tq, ATT_VD), F32))
    carry = lax.fori_loop(0, qi, lambda c, cr: chunk(c, cr, False), init)
    _, l, acc = chunk(qi, carry, True)
    lam = _lambda_value(lam_ref, lam_init)
    o = acc / l
    o = o[:tq] - lam * o[tq:]
    o_ref[...] = _rms(o, sub_ref[...], SUBLN_EPS) * (1.0 - lam_init)


def _flash_prompt(qkv, b, t, lam_p, subln, lam_init, tq):
    d = qkv.shape[-1]
    return pl.pallas_call(
        functools.partial(_flash_body, lam_init=lam_init, tq=tq),
        grid=(b, ATT_HEADS, t // tq),
        in_specs=[
            pl.BlockSpec((None, None, tq, ATT_VD), lambda i, h, j: (0, i, j, h)),
            pl.BlockSpec((None, None, t, ATT_VD), lambda i, h, j: (1, i, 0, h)),
            pl.BlockSpec((None, None, t, ATT_VD), lambda i, h, j: (2, i, 0, h)),
            pl.BlockSpec((4, ATT_DH), lambda i, h, j: (0, 0)),
            pl.BlockSpec((1, ATT_VD), lambda i, h, j: (0, 0)),
        ],
        out_specs=pl.BlockSpec((None, tq, ATT_VD), lambda i, h, j: (i, j, h)),
        out_shape=jax.ShapeDtypeStruct((b, t, d), F32),
        compiler_params=_cparams("parallel", "parallel", "arbitrary"),
        name="flash_prompt",
    )(qkv, qkv, qkv, lam_p, subln.reshape(1, ATT_VD))


def _paged_body(*refs, lam_init, pages_per_step):
    pp = pages_per_step
    pt_ref = refs[0]
    q_ref, kn_ref, vn_ref = refs[1:4]
    kp_refs = refs[4:4 + pp]
    vp_refs = refs[4 + pp:4 + 2 * pp]
    lam_ref, sub_ref, o_ref, qm_ref, m_ref, l_ref, acc_ref = refs[4 + 2 * pp:]
    del pt_ref
    step = pl.program_id(1)
    s_len = o_ref.shape[0]
    grp = q_ref.shape[0]
    nt = (((1,), (1,)), ((), ()))
    heads = range(ATT_HEADS)

    def head_cols(x, h):
        return x[:, h * ATT_VD:(h + 1) * ATT_VD]

    @pl.when(step == 0)
    def _():
        q = q_ref[...] * ATT_SCALE
        r = lax.broadcasted_iota(jnp.int32, q.shape, 0)
        c = lax.broadcasted_iota(jnp.int32, q.shape, 1)
        q = jnp.where((c % ATT_VD) // ATT_DH == r // s_len, q, 0.0).astype(BF16)
        qm_ref[...] = jnp.concatenate([head_cols(q, h) for h in heads], axis=0)
        m_ref[...] = jnp.full(m_ref.shape, NEG_INF, F32)
        l_ref[...] = jnp.zeros(l_ref.shape, F32)
        acc_ref[...] = jnp.zeros(acc_ref.shape, F32)

    qm = qm_ref[...]

    def head_rows(x, h):
        return x[h * grp:(h + 1) * grp]

    def merge(s, weighted_values):
        m = m_ref[...]
        m_new = jnp.maximum(m, jnp.max(s, axis=-1, keepdims=True))
        alpha = jnp.exp(m - m_new)
        p = jnp.exp(s - m_new)
        l_ref[...] = l_ref[...] * alpha + jnp.sum(p, axis=-1, keepdims=True)
        acc_ref[...] = acc_ref[...] * alpha + weighted_values(p.astype(BF16))
        m_ref[...] = m_new

    page_rows = PAGE_SIZE * ATT_HEADS
    own_head = (lax.broadcasted_iota(jnp.int32, (qm.shape[0], page_rows), 0) // grp
                == lax.broadcasted_iota(jnp.int32, (qm.shape[0], page_rows), 1) % ATT_HEADS)
    for kp_ref, vp_ref in zip(kp_refs, vp_refs):
        ka = kp_ref[...].reshape(page_rows, ATT_VD).astype(BF16)
        va = vp_ref[...].reshape(page_rows, ATT_VD).astype(BF16)
        s = lax.dot_general(qm, ka, nt, preferred_element_type=F32)
        merge(jnp.where(own_head, s, NEG_INF), lambda pb: jnp.dot(pb, va, preferred_element_type=F32))

    @pl.when(step == pl.num_programs(1) - 1)
    def _():
        kn = kn_ref[...].astype(BF16)
        vn = vn_ref[...].astype(BF16)
        s = jnp.concatenate([lax.dot_general(head_rows(qm, h), head_cols(kn, h), nt, preferred_element_type=F32)
                             for h in heads], axis=0)
        t = lax.broadcasted_iota(jnp.int32, s.shape, 0) % s_len
        c = lax.broadcasted_iota(jnp.int32, s.shape, 1)
        s = jnp.where(c <= t, s, NEG_INF)
        merge(s, lambda pb: jnp.concatenate(
            [jnp.dot(head_rows(pb, h), head_cols(vn, h), preferred_element_type=F32) for h in heads], axis=0))
        o = acc_ref[...] / l_ref[...]
        lam = _lambda_value(lam_ref, lam_init)
        for h in heads:
            blk = head_rows(o, h)
            oh = blk[:s_len] - lam * blk[s_len:]
            o_ref[:, h * ATT_VD:(h + 1) * ATT_VD] = _rms(oh, sub_ref[...], SUBLN_EPS) * (1.0 - lam_init)


def _paged_sample(q, k_new, v_new, cache_k, cache_v, layer, page_table, lam_p, subln, lam_init, pages_per_step):
    b, s, d = q.shape
    n_pages = page_table.shape[1]
    pp = pages_per_step
    grp = 2 * s
    rows = ATT_HEADS * grp
    q2 = jnp.concatenate([q, q], axis=1)
    k_new, v_new = (jnp.pad(z, ((0, 0), (0, grp - s), (0, 0))) for z in (k_new, v_new))
    tok = pl.BlockSpec((None, grp, d), lambda i, j, pt: (i, 0, 0))

    def page_spec(o):
        return pl.BlockSpec((None, None, PAGE_SIZE, ATT_HEADS, ATT_VD),
                            lambda i, j, pt: (layer, pt[i, j * pp + o], 0, 0, 0))

    grid_spec = pltpu.PrefetchScalarGridSpec(
        num_scalar_prefetch=1,
        grid=(b, n_pages // pp),
        in_specs=[tok, tok, tok] + [page_spec(o) for o in range(pp)] + [page_spec(o) for o in range(pp)] + [
            pl.BlockSpec((4, ATT_DH), lambda i, j, pt: (0, 0)),
            pl.BlockSpec((1, ATT_VD), lambda i, j, pt: (0, 0)),
        ],
        out_specs=pl.BlockSpec((None, s, d), lambda i, j, pt: (i, 0, 0)),
        scratch_shapes=[pltpu.VMEM((rows, ATT_VD), BF16), pltpu.VMEM((rows, 1), F32), pltpu.VMEM((rows, 1), F32),
                        pltpu.VMEM((rows, ATT_VD), F32)],
    )
    return pl.pallas_call(
        functools.partial(_paged_body, lam_init=lam_init, pages_per_step=pp),
        grid_spec=grid_spec,
        out_shape=jax.ShapeDtypeStruct((b, s, d), F32),
        compiler_params=_cparams("parallel", "arbitrary"),
        name="paged_sample",
    )(page_table, q2, k_new, v_new, *([cache_k] * pp), *([cache_v] * pp), lam_p, subln.reshape(1, ATT_VD))


def kernel(x_prompt, x_sample, state_wkv, state_shift, state_conv, cache_k, cache_v, page_table, norm_mix, norm_ffn, norm_final, rwkv_mu, rwkv_w_rkv, rwkv_w0, rwkv_w1, rwkv_w2, rwkv_a0, rwkv_a1, rwkv_a2, rwkv_v0, rwkv_v1, rwkv_v2, rwkv_g1, rwkv_g2, rwkv_k_k, rwkv_k_a, rwkv_r_k, rwkv_ln_g, rwkv_ln_b, rwkv_w_o, conv_w_pw1, conv_b_pw1, conv_w_dw, conv_b_dw, conv_ln_g, conv_ln_b, conv_w_pw2, conv_b_pw2, attn_w_qkv, attn_lambda, attn_subln, attn_w_o, ffn_w13, ffn_w2, moe_router, moe_w13, moe_w2):
    d = D_MODEL
    bf = lambda z: z.astype(BF16)
    w_rkv, w1, w2, a1, a2, v1, v2, g1, g2, w_o = map(
        bf, (rwkv_w_rkv, rwkv_w1, rwkv_w2, rwkv_a1, rwkv_a2, rwkv_v1, rwkv_v2, rwkv_g1, rwkv_g2, rwkv_w_o))
    c_pw1, c_pw2 = bf(conv_w_pw1), bf(conv_w_pw2)
    a_qkv, a_wo = bf(attn_w_qkv), bf(attn_w_o)
    f_w13, f_w2 = bf(ffn_w13), bf(ffn_w2)
    m_w13, m_w2 = bf(moe_w13), bf(moe_w2)
    d_ff = ffn_w2.shape[1]
    n_pool = cache_k.shape[1]

    def run(x3, wkv_in, shift_in, conv_in, sample):
        b, t, _ = x3.shape
        m = b * t
        x = x3.reshape(m, d)
        wkv_out, shift_out, conv_out, k_out, v_out = [], [], [], [], []
        v_first = None
        for i in range(DEPTH):
            kind, j = i % 3, i // 3
            if kind == 0:
                vres = None if j == 0 else (rwkv_v0[j - 1], v1[j - 1], v2[j - 1])
                x, sh, s_new, v_first = _rwkv_layer(
                    x, b, t, shift_in[j], wkv_in[j], v_first, vres, norm_mix[i], rwkv_mu[j], w_rkv[j],
                    rwkv_w0[j], w1[j], w2[j], rwkv_a0[j], a1[j], a2[j], g1[j], g2[j], rwkv_k_k[j],
                    rwkv_k_a[j], rwkv_r_k[j], rwkv_ln_g[j], rwkv_ln_b[j], w_o[j])
                wkv_out.append(s_new)
                shift_out.append(sh)
            elif kind == 1:
                conv_args = (norm_mix[i], c_pw1[j], conv_b_pw1[j], conv_w_dw[j], conv_b_dw[j], conv_ln_g[j],
                             conv_ln_b[j], c_pw2[j], conv_b_pw2[j])
                if sample:
                    y3, buf = _conv_sample(x.reshape(b, t, d), conv_in[j], *conv_args, 32)
                else:
                    y3, buf = _conv_prompt(x.reshape(b, t, d), conv_in[j], *conv_args, 256)
                x = y3.reshape(m, d)
                conv_out.append(buf)
            else:
                lam_init = 0.8 - 0.6 * math.exp(-0.3 * i)
                qkv, qkv_b = _matmul(x, a_qkv[j], norm_g=norm_mix[i], col_tile=d, split_out=True,
                                     bf16_copy=True)
                if sample:
                    q3, k3, v3 = (qkv[n].reshape(b, t, d) for n in range(3))
                    o = _paged_sample(q3, k3, v3, cache_k, cache_v, j, page_table, attn_lambda[j],
                                      attn_subln[j], lam_init, 4)
                else:
                    o = _flash_prompt(qkv_b.reshape(3, b, t, d), b, t, attn_lambda[j], attn_subln[j], lam_init, 256)
                x = _matmul(o.reshape(m, d), a_wo[j], res=x)
                k_out.append(qkv[1].reshape(b, t, ATT_HEADS, 2 * ATT_DH))
                v_out.append(qkv[2].reshape(b, t, ATT_HEADS, ATT_VD))
            if i % 2 == 0:
                x = _ffn_dense(x, norm_ffn[i], f_w13[i // 2], f_w2[i // 2], d_ff // 2)
            else:
                x = _moe(x, norm_ffn[i], moe_router[i // 2], m_w13[i // 2], m_w2[i // 2], 512)
        y = _rmsnorm(x, norm_final).reshape(b, t, d)
        return (y, jnp.stack(wkv_out), jnp.stack(shift_out), jnp.stack(conv_out), jnp.stack(k_out),
                jnp.stack(v_out))

    bp = x_prompt.shape[0]
    wkv0 = jnp.zeros((state_wkv.shape[0], bp) + state_wkv.shape[2:], state_wkv.dtype)
    shift0 = jnp.zeros((state_shift.shape[0], bp, d), x_prompt.dtype)
    conv0 = jnp.zeros((state_conv.shape[0], bp, CONV_BUF, d), x_prompt.dtype)
    y_p, wkv_p, sh_p, conv_p, k_p, v_p = run(x_prompt, wkv0, shift0, conv0, False)
    y_s, wkv_s, sh_s, conv_s, k_s, v_s = run(x_sample, state_wkv, state_shift, state_conv, True)
    return (y_p, y_s, wkv_p, wkv_s, sh_p, sh_s, conv_p, conv_s, k_p, k_s, v_p, v_s)
```

```python
import functools
import math

import jax
import jax.numpy as jnp
from jax import lax
from jax.experimental import pallas as pl
from jax.experimental.pallas import tpu as pltpu

F32 = jnp.float32
BF16 = jnp.bfloat16

D_MODEL = 1024
DEPTH = 4
PAGE_SIZE = 128
RWKV_HEAD = 64
RWKV_HEADS = D_MODEL // RWKV_HEAD
GN_EPS = 64e-5
CONV_WIDTH = 31
CONV_BUF = CONV_WIDTH - 1
LN_EPS = 1e-5
ATT_HEADS = 8
ATT_DH = 64
ATT_VD = 128
ATT_SCALE = ATT_DH ** -0.5
SUBLN_EPS = 1e-5
N_EXPERTS = 8
TOP_K = 2
RMS_EPS = 1e-6

V7X_LANES = 128
V7X_SUBLANES = 8
V7X_VMEM_LIMIT_BYTES = 56 * 1024 * 1024

ROW_TILE = 512
EXPERT_TILE_MIN = 128
EXPERT_TILE_MAX = 1024
CONV_HALO = 32
NEG_INF = float("-inf")


def _cparams(*sem):
    return pltpu.CompilerParams(dimension_semantics=sem, vmem_limit_bytes=V7X_VMEM_LIMIT_BYTES)


def _rms(x, g, eps=RMS_EPS):
    return x * lax.rsqrt(jnp.mean(x * x, axis=-1, keepdims=True) + eps) * g


def _sigmoid(x):
    return 1.0 / (1.0 + jnp.exp(-x))


def _silu(x):
    return x * _sigmoid(x)


def _row_tile(m):
    return min(ROW_TILE, m)


def _rms_body(x_ref, g_ref, o_ref):
    o_ref[...] = _rms(x_ref[...], g_ref[...]).astype(o_ref.dtype)


def _rmsnorm(x, g, out_dtype=F32):
    m, d = x.shape
    tm = _row_tile(m)
    return pl.pallas_call(
        _rms_body,
        grid=(m // tm,),
        in_specs=[pl.BlockSpec((tm, d), lambda i: (i, 0)), pl.BlockSpec((1, d), lambda i: (0, 0))],
        out_specs=pl.BlockSpec((tm, d), lambda i: (i, 0)),
        out_shape=jax.ShapeDtypeStruct((m, d), out_dtype),
        compiler_params=_cparams("parallel"),
        name="rmsnorm",
    )(x, g.reshape(1, d))


def _mm_body(*refs, norm, mul, bias, act, res, glu, bf16_copy):
    it = iter(refs)
    x_ref = next(it)
    g_ref = next(it) if norm else None
    m_ref = next(it) if mul else None
    w_ref = next(it)
    w2_ref = next(it) if glu else None
    b_ref = next(it) if bias else None
    b2_ref = next(it) if (bias and glu) else None
    r_ref = next(it) if res else None
    o_ref = next(it)
    ob_ref = next(it) if bf16_copy else None
    xs_ref = next(it)

    @pl.when(pl.program_id(1) == 0)
    def _():
        x = x_ref[...].astype(F32)
        if norm:
            x = _rms(x, g_ref[...])
        if mul:
            x = x * m_ref[...]
        xs_ref[...] = x.astype(BF16)

    xs = xs_ref[...]
    y = jnp.dot(xs, w_ref[...], preferred_element_type=F32)
    if bias:
        y = y + b_ref[...]
    if glu:
        y2 = jnp.dot(xs, w2_ref[...], preferred_element_type=F32)
        if bias:
            y2 = y2 + b2_ref[...]
        y = y * _sigmoid(y2)
    if act == "tanh":
        y = jnp.tanh(y)
    elif act == "sigmoid":
        y = _sigmoid(y)
    if res:
        y = y + r_ref[...]
    o_ref[...] = y.astype(o_ref.dtype)
    if bf16_copy:
        ob_ref[...] = y.astype(BF16)


def _matmul(x, w, *, norm_g=None, mul=None, bias=None, act=None, res=None, glu=False,
            col_tile=None, split_out=False, bf16_copy=False):
    m, k = x.shape
    n = w.shape[1]
    n_out = n // 2 if glu else n
    tm = _row_tile(m)
    tn = col_tile or n_out
    nj = n_out // tn
    in_specs = [pl.BlockSpec((tm, k), lambda i, j: (i, 0))]
    args = [x]
    if norm_g is not None:
        in_specs.append(pl.BlockSpec((1, k), lambda i, j: (0, 0)))
        args.append(norm_g.reshape(1, k))
    if mul is not None:
        in_specs.append(pl.BlockSpec((tm, k), lambda i, j: (i, 0)))
        args.append(mul)
    in_specs.append(pl.BlockSpec((k, tn), lambda i, j: (0, j)))
    args.append(w)
    if glu:
        in_specs.append(pl.BlockSpec((k, tn), lambda i, j: (0, j + nj)))
        args.append(w)
    if bias is not None:
        b2d = bias.reshape(1, n)
        in_specs.append(pl.BlockSpec((1, tn), lambda i, j: (0, j)))
        args.append(b2d)
        if glu:
            in_specs.append(pl.BlockSpec((1, tn), lambda i, j: (0, j + nj)))
            args.append(b2d)
    if res is not None:
        in_specs.append(pl.BlockSpec((tm, tn), lambda i, j: (i, j)))
        args.append(res)
    if split_out:
        out_shape = jax.ShapeDtypeStruct((nj, m, tn), F32)
        out_spec = pl.BlockSpec((None, tm, tn), lambda i, j: (j, i, 0))
    else:
        out_shape = jax.ShapeDtypeStruct((m, n_out), F32)
        out_spec = pl.BlockSpec((tm, tn), lambda i, j: (i, j))
    if bf16_copy:
        out_shape = [out_shape, jax.ShapeDtypeStruct(out_shape.shape, BF16)]
        out_spec = [out_spec, out_spec]
    body = functools.partial(_mm_body, norm=norm_g is not None, mul=mul is not None,
                             bias=bias is not None, act=act, res=res is not None, glu=glu,
                             bf16_copy=bf16_copy)
    return pl.pallas_call(
        body,
        grid=(m // tm, nj),
        in_specs=in_specs,
        out_specs=out_spec,
        out_shape=out_shape,
        scratch_shapes=[pltpu.VMEM((tm, k), BF16)],
        compiler_params=_cparams("parallel", "arbitrary"),
        name="matmul",
    )(*args)


def _qkv_body(x_ref, g_ref, w_ref, q_ref, k_ref, v_ref, qb_ref, kb_ref, vb_ref):
    d = q_ref.shape[1]
    xs = _rms(x_ref[...], g_ref[...]).astype(BF16)
    y = jnp.dot(xs, w_ref[...], preferred_element_type=F32)
    for n, (o_ref, ob_ref) in enumerate(((q_ref, qb_ref), (k_ref, kb_ref), (v_ref, vb_ref))):
        part = y[:, n * d:(n + 1) * d]
        o_ref[...] = part
        ob_ref[...] = part.astype(BF16)


def _qkv_proj(x, g, w):
    m, d = x.shape
    tm = min(256, m)
    tok = pl.BlockSpec((tm, d), lambda i: (i, 0))
    return pl.pallas_call(
        _qkv_body,
        grid=(m // tm,),
        in_specs=[tok, pl.BlockSpec((1, d), lambda i: (0, 0)), pl.BlockSpec(w.shape, lambda i: (0, 0))],
        out_specs=[tok] * 6,
        out_shape=[jax.ShapeDtypeStruct((m, d), F32)] * 3 + [jax.ShapeDtypeStruct((m, d), BF16)] * 3,
        compiler_params=_cparams("parallel"),
        name="qkv_proj",
    )(x, g.reshape(1, d), w)


def _rows_from_tiles(ref):
    rows, sub, lanes = ref.shape
    x = jnp.swapaxes(ref[...].reshape(rows // sub, sub, sub, lanes), 1, 2)
    return jnp.concatenate([x[:, s].reshape(rows, lanes) for s in range(sub)], axis=-1)


def _rows_to_tiles(ref, val):
    rows, sub, lanes = ref.shape
    x = jnp.stack([val[:, s * lanes:(s + 1) * lanes].reshape(rows // sub, sub, lanes) for s in range(sub)],
                  axis=1)
    ref[...] = jnp.swapaxes(x, 1, 2).reshape(rows, sub, lanes)


def _ffn_body(*refs, moe):
    if moe:
        te_ref, tv_ref, x_ref, g_ref, wg_ref, wu_ref, wd_ref, o_ref, hs_ref, acc_ref = refs
    else:
        x_ref, g_ref, wg_ref, wu_ref, wd_ref, o_ref, hs_ref, acc_ref = refs
    j = pl.program_id(1)
    nj = pl.num_programs(1)

    def compute():
        @pl.when(j == 0)
        def _():
            x = _rows_from_tiles(x_ref) if moe else x_ref[...]
            hs_ref[...] = _rms(x, g_ref[...]).astype(BF16)
            acc_ref[...] = jnp.zeros_like(acc_ref)

        h = hs_ref[...]
        gate = jnp.dot(h, wg_ref[...], preferred_element_type=F32)
        up = jnp.dot(h, wu_ref[...], preferred_element_type=F32)
        a = (_silu(gate) * up).astype(BF16)
        acc_ref[...] += jnp.dot(a, wd_ref[...], preferred_element_type=F32)

        @pl.when(j == nj - 1)
        def _():
            if moe:
                _rows_to_tiles(o_ref, acc_ref[...])
            else:
                o_ref[...] = x_ref[...] + acc_ref[...]

    if moe:
        valid = tv_ref[pl.program_id(0)] > 0
        pl.when(valid)(compute)

        @pl.when(jnp.logical_and(jnp.logical_not(valid), j == nj - 1))
        def _():
            o_ref[...] = jnp.zeros_like(o_ref)
    else:
        compute()


def _ffn_dense(x, g, w13, w2, ff_tile):
    m, d = x.shape
    f = w2.shape[0]
    tm = _row_tile(m)
    nf = f // ff_tile
    return pl.pallas_call(
        functools.partial(_ffn_body, moe=False),
        grid=(m // tm, nf),
        in_specs=[
            pl.BlockSpec((tm, d), lambda i, j: (i, 0)),
            pl.BlockSpec((1, d), lambda i, j: (0, 0)),
            pl.BlockSpec((d, ff_tile), lambda i, j: (0, j)),
            pl.BlockSpec((d, ff_tile), lambda i, j: (0, j + nf)),
            pl.BlockSpec((ff_tile, d), lambda i, j: (j, 0)),
        ],
        out_specs=pl.BlockSpec((tm, d), lambda i, j: (i, 0)),
        out_shape=jax.ShapeDtypeStruct((m, d), F32),
        scratch_shapes=[pltpu.VMEM((tm, d), BF16), pltpu.VMEM((tm, d), F32)],
        compiler_params=_cparams("parallel", "arbitrary"),
        name="ffn_dense",
    )(x, g.reshape(1, d), w13, w13, w2)


def _ffn_experts(xs, g, w13, w2, tile_expert, tile_valid, tm, ff_tile):
    s, sub, lanes = xs.shape
    d = sub * lanes
    f = w2.shape[1]
    nf = f // ff_tile
    tok = pl.BlockSpec((tm, sub, lanes), lambda i, j, te, tv: (i, 0, 0))
    grid_spec = pltpu.PrefetchScalarGridSpec(
        num_scalar_prefetch=2,
        grid=(s // tm, nf),
        in_specs=[
            tok,
            pl.BlockSpec((1, d), lambda i, j, te, tv: (0, 0)),
            pl.BlockSpec((None, d, ff_tile), lambda i, j, te, tv: (te[i], 0, j)),
            pl.BlockSpec((None, d, ff_tile), lambda i, j, te, tv: (te[i], 0, j + nf)),
            pl.BlockSpec((None, ff_tile, d), lambda i, j, te, tv: (te[i], j, 0)),
        ],
        out_specs=tok,
        scratch_shapes=[pltpu.VMEM((tm, d), BF16), pltpu.VMEM((tm, d), F32)],
    )
    return pl.pallas_call(
        functools.partial(_ffn_body, moe=True),
        grid_spec=grid_spec,
        out_shape=jax.ShapeDtypeStruct((s, sub, lanes), F32),
        compiler_params=_cparams("parallel", "arbitrary"),
        name="ffn_experts",
    )(tile_expert, tile_valid, xs, g.reshape(1, d), w13, w13, w2)


def _router_body(x_ref, g_ref, whi_ref, wlo_ref, o_ref, g1_ref, g2_ref):
    h = _rms(x_ref[...], g_ref[...])
    hhi = h.astype(BF16)
    hlo = (h - hhi.astype(F32)).astype(BF16)
    whi = whi_ref[...]
    lg = (jnp.dot(hhi, whi, preferred_element_type=F32)
          + jnp.dot(hlo, whi, preferred_element_type=F32)
          + jnp.dot(hhi, wlo_ref[...], preferred_element_type=F32))
    lane = lax.broadcasted_iota(jnp.int32, lg.shape, 1)
    lg = jnp.where(lane < N_EXPERTS, lg, NEG_INF)
    m1 = jnp.max(lg, axis=-1, keepdims=True)
    i1 = jnp.min(jnp.where(lg == m1, lane, V7X_LANES), axis=-1, keepdims=True)
    lg2 = jnp.where(lane == i1, NEG_INF, lg)
    m2 = jnp.max(lg2, axis=-1, keepdims=True)
    i2 = jnp.min(jnp.where(lg2 == m2, lane, V7X_LANES), axis=-1, keepdims=True)
    e2 = jnp.exp(m2 - m1)
    g1 = 1.0 / (1.0 + e2)
    g2 = e2 / (1.0 + e2)
    out = jnp.where(lane == 0, i1.astype(F32),
                    jnp.where(lane == 1, i2.astype(F32),
                              jnp.where(lane == 2, g1, jnp.where(lane == 3, g2, 0.0))))
    o_ref[...] = out
    g1_ref[:, 0, :] = jnp.broadcast_to(g1, lg.shape)
    g2_ref[:, 0, :] = jnp.broadcast_to(g2, lg.shape)


def _router(x, g, w_router):
    m, d = x.shape
    tm = _row_tile(m)
    wpad = jnp.zeros((d, V7X_LANES), F32).at[:, :N_EXPERTS].set(w_router)
    whi = wpad.astype(BF16)
    wlo = (wpad - whi.astype(F32)).astype(BF16)
    gate_spec = pl.BlockSpec((tm, 1, V7X_LANES), lambda i: (i, 0, 0))
    gate_shape = jax.ShapeDtypeStruct((m, 1, V7X_LANES), F32)
    return pl.pallas_call(
        _router_body,
        grid=(m // tm,),
        in_specs=[
            pl.BlockSpec((tm, d), lambda i: (i, 0)),
            pl.BlockSpec((1, d), lambda i: (0, 0)),
            pl.BlockSpec((d, V7X_LANES), lambda i: (0, 0)),
            pl.BlockSpec((d, V7X_LANES), lambda i: (0, 0)),
        ],
        out_specs=[pl.BlockSpec((tm, V7X_LANES), lambda i: (i, 0)), gate_spec, gate_spec],
        out_shape=[jax.ShapeDtypeStruct((m, V7X_LANES), F32), gate_shape, gate_shape],
        compiler_params=_cparams("parallel"),
        name="router",
    )(x, g.reshape(1, d), whi, wlo)


DMA_UNROLL = 8


def _gather_body(src_ref, x_hbm, o_ref, sem):
    tg = o_ref.shape[0]
    base = pl.program_id(0) * tg

    def tile_copy(r):
        return pltpu.make_async_copy(x_hbm.at[pl.ds(src_ref[base + r], 1)], o_ref.at[pl.ds(r, 1)], sem)

    def start(i, c):
        for u in range(DMA_UNROLL):
            tile_copy(i * DMA_UNROLL + u).start(priority=u % 2)
        return c

    def wait(i, c):
        for u in range(DMA_UNROLL):
            tile_copy(i * DMA_UNROLL + u).wait()
        return c

    lax.fori_loop(0, tg // DMA_UNROLL, start, 0)
    lax.fori_loop(0, tg // DMA_UNROLL, wait, 0)


def _gather_tokens(x3, src, tg):
    s = src.shape[0]
    _, sub, lanes = x3.shape
    grid_spec = pltpu.PrefetchScalarGridSpec(
        num_scalar_prefetch=1,
        grid=(s // tg,),
        in_specs=[pl.BlockSpec(memory_space=pl.ANY)],
        out_specs=pl.BlockSpec((tg, sub, lanes), lambda i, src: (i, 0, 0)),
        scratch_shapes=[pltpu.SemaphoreType.DMA(())],
    )
    return pl.pallas_call(
        _gather_body,
        grid_spec=grid_spec,
        out_shape=jax.ShapeDtypeStruct((s, sub, lanes), x3.dtype),
        compiler_params=_cparams("arbitrary"),
        name="moe_gather",
    )(src, x3)


def _combine_body(dest_ref, x_ref, g1_ref, g2_ref, y_hbm, o_ref, buf, sem):
    tc = o_ref.shape[0]
    base = pl.program_id(0) * tc

    def tile_copy(r, k):
        return pltpu.make_async_copy(y_hbm.at[pl.ds(dest_ref[TOP_K * (base + r) + k], 1)],
                                     buf.at[k, pl.ds(r, 1)], sem)

    def start(i, c):
        for u in range(DMA_UNROLL):
            for k in range(TOP_K):
                tile_copy(i * DMA_UNROLL + u, k).start(priority=k)
        return c

    def wait(i, c):
        for u in range(DMA_UNROLL):
            for k in range(TOP_K):
                tile_copy(i * DMA_UNROLL + u, k).wait()
        return c

    lax.fori_loop(0, tc // DMA_UNROLL, start, 0)
    lax.fori_loop(0, tc // DMA_UNROLL, wait, 0)
    o_ref[...] = x_ref[...] + g1_ref[...] * buf[0] + g2_ref[...] * buf[1]


def _moe_combine(x3, g1, g2, y3, dest, tc):
    m, sub, lanes = x3.shape
    tok = pl.BlockSpec((tc, sub, lanes), lambda i, dest: (i, 0, 0))
    gate = pl.BlockSpec((tc, 1, lanes), lambda i, dest: (i, 0, 0))
    grid_spec = pltpu.PrefetchScalarGridSpec(
        num_scalar_prefetch=1,
        grid=(m // tc,),
        in_specs=[tok, gate, gate, pl.BlockSpec(memory_space=pl.ANY)],
        out_specs=tok,
        scratch_shapes=[pltpu.VMEM((TOP_K, tc, sub, lanes), F32), pltpu.SemaphoreType.DMA(())],
    )
    return pl.pallas_call(
        _combine_body,
        grid_spec=grid_spec,
        out_shape=jax.ShapeDtypeStruct((m, sub, lanes), F32),
        compiler_params=_cparams("arbitrary"),
        name="moe_combine",
    )(dest, x3, g1, g2, y3)


def _moe(x, g, w_router, w13, w2, ff_tile):
    m, d = x.shape
    tm = max(EXPERT_TILE_MIN, min(EXPERT_TILE_MAX, m // 16))
    r, g1, g2 = _router(x, g, w_router)
    e_flat = r[:, :TOP_K].astype(jnp.int32).reshape(-1)
    n_slots = m * TOP_K
    n_tiles = n_slots // tm + N_EXPERTS
    s_pad = n_tiles * tm
    onehot = (e_flat[:, None] == jnp.arange(N_EXPERTS, dtype=jnp.int32)[None, :]).astype(jnp.int32)
    rank = jnp.sum((jnp.cumsum(onehot, axis=0) - 1) * onehot, axis=1)
    counts = jnp.sum(onehot, axis=0)
    padded = ((counts + tm - 1) // tm) * tm
    ends = jnp.cumsum(padded)
    starts = ends - padded
    dest = (jnp.sum(onehot * starts[None, :], axis=1) + rank).astype(jnp.int32)
    token = jnp.arange(n_slots, dtype=jnp.int32) // TOP_K
    src = (jnp.arange(s_pad, dtype=jnp.int32) % m).at[dest].set(token)
    tile_start = jnp.arange(n_tiles, dtype=jnp.int32) * tm
    tile_valid = (tile_start < ends[-1]).astype(jnp.int32)
    tile_expert = jnp.sum((tile_start[:, None] >= ends[None, :]).astype(jnp.int32), axis=1)
    last_expert = jnp.max(jnp.where(tile_valid > 0, tile_expert, 0))
    tile_expert = jnp.where(tile_valid > 0, tile_expert, last_expert).astype(jnp.int32)
    x3 = x.reshape(m, d // V7X_LANES, V7X_LANES)
    xs = _gather_tokens(x3, src, tm)
    ys = _ffn_experts(xs, g, w13, w2, tile_expert, tile_valid, tm, ff_tile)
    return _moe_combine(x3, g1, g2, ys, dest, min(256, m)).reshape(m, d)


def _wkv_body(r_ref, wl_ref, k_ref, v_ref, a_ref, kk_ref, ka_ref, rk_ref, lg_ref, lb_ref, s0_ref,
              y_ref, s_ref, w_s, a_s, b_s, k_s, r_s):
    tc = pl.program_id(1)
    n = RWKV_HEAD
    tt = r_ref.shape[0]

    @pl.when(tc == 0)
    def _():
        s_ref[...] = s0_ref[...]

    kkw = kk_ref[...]
    kaw = ka_ref[...]
    rkw = rk_ref[...]
    lgw = lg_ref[...]
    lbw = lb_ref[...]

    def unit_key(t):
        kk = k_ref[t] * kkw
        return kk / jnp.maximum(jnp.sqrt(jnp.sum(kk * kk, axis=0, keepdims=True)), 1e-12)

    def bc(ref, i):
        return jnp.broadcast_to(ref[pl.ds(i, 1), :], (n, V7X_LANES))

    a_s[...] = -unit_key(0)
    sa0 = jnp.zeros((n, V7X_LANES), F32)
    for i in range(n):
        sa0 = sa0 + s_ref[i] * bc(a_s, i)

    def step(t, sa):
        r = r_ref[t]
        k = k_ref[t]
        v = v_ref[t]
        a = a_ref[t]
        k2 = k * (1.0 + (a - 1.0) * kaw)
        w_s[...] = jnp.exp(-jnp.exp(wl_ref[t]))
        b_s[...] = unit_key(t) * a
        k_s[...] = k2
        r_s[...] = r
        a_s[...] = -unit_key(jnp.minimum(t + 1, tt - 1))
        def key_group(gi, acc):
            y, sa_next = acc
            base = pl.multiple_of(gi * V7X_SUBLANES, V7X_SUBLANES)
            for u in range(V7X_SUBLANES):
                i = base + u
                s_new = s_ref[i] * bc(w_s, i) + sa * bc(b_s, i) + v * bc(k_s, i)
                s_ref[i] = s_new
                y = y + s_new * bc(r_s, i)
                sa_next = sa_next + s_new * bc(a_s, i)
            return y, sa_next

        zero = jnp.zeros((n, V7X_LANES), F32)
        y, sa_next = lax.fori_loop(0, n // V7X_SUBLANES, key_group, (zero, zero))
        ym = jnp.mean(y, axis=0, keepdims=True)
        yc = y - ym
        yn = yc * lax.rsqrt(jnp.mean(yc * yc, axis=0, keepdims=True) + GN_EPS)
        bonus = jnp.sum(r * k2 * rkw, axis=0, keepdims=True) * v
        y_ref[t] = yn * lgw + lbw + bonus
        return sa_next

    lax.fori_loop(0, tt, step, sa0)


def _wkv(r, wl, k, v, a, k_k, k_a, r_k, ln_g, ln_b, s0, time_tile):
    t, n, l = r.shape
    tt = min(time_tile, t)
    seq = pl.BlockSpec((tt, n, V7X_LANES), lambda i, j: (j, 0, i))
    par = pl.BlockSpec((n, V7X_LANES), lambda i, j: (0, i))
    st = pl.BlockSpec((n, n, V7X_LANES), lambda i, j: (0, 0, i))
    return pl.pallas_call(
        _wkv_body,
        grid=(l // V7X_LANES, t // tt),
        in_specs=[seq, seq, seq, seq, seq, par, par, par, par, par, st],
        out_specs=[seq, st],
        out_shape=[jax.ShapeDtypeStruct((t, n, l), F32), jax.ShapeDtypeStruct((n, n, l), F32)],
        scratch_shapes=[pltpu.VMEM((n, V7X_LANES), F32)] * 5,
        compiler_params=_cparams("parallel", "arbitrary"),
        name="wkv7",
    )(r, wl, k, v, a, k_k, k_a, r_k, ln_g, ln_b, s0)


def _softplus(z):
    return jnp.maximum(z, 0.0) + jnp.log1p(jnp.exp(-jnp.abs(z)))


def _rwkv_proj_body(*refs, has_vres):
    (h_ref, hp_ref, mu_ref, wr_ref, wk_ref, wv_ref, w0_ref, w1_ref, w2_ref,
     a0_ref, a1_ref, a2_ref, g1_ref, g2_ref) = refs[:14]
    n_in = 14
    if has_vres:
        vf_ref, v0_ref, v1_ref, v2_ref = refs[14:18]
        n_in = 18
    r_ref, k_ref, v_ref, wl_ref, a_ref, g_ref = refs[n_in:]
    h = h_ref[...]
    xx = hp_ref[...] - h

    def mix(i):
        return (h + xx * mu_ref[pl.ds(i, 1), :]).astype(BF16)

    def dot(x, w_ref):
        return jnp.dot(x.astype(BF16), w_ref[...], preferred_element_type=F32)

    r_ref[...] = dot(mix(0), wr_ref)
    z = w0_ref[...] + dot(jnp.tanh(dot(mix(1), w1_ref)), w2_ref)
    wl_ref[...] = -_softplus(-z) - 0.5
    k_ref[...] = dot(mix(2), wk_ref)
    xv = mix(3)
    v = dot(xv, wv_ref)
    if has_vres:
        v = v + (vf_ref[...] - v) * _sigmoid(v0_ref[...] + dot(dot(xv, v1_ref), v2_ref))
    v_ref[...] = v
    a_ref[...] = _sigmoid(a0_ref[...] + dot(dot(mix(4), a1_ref), a2_ref))
    g_ref[...] = dot(_sigmoid(dot(mix(5), g1_ref)), g2_ref)


def _rwkv_proj(h, h_prev, mu, w_rkv, w0, w1, w2, a0, a1, a2, g1, g2, v_first, vres):
    m, c = h.shape
    tm = min(256, m)
    tok = pl.BlockSpec((tm, c), lambda i: (i, 0))

    def whole(z):
        return pl.BlockSpec(z.shape, lambda i: (0,) * z.ndim)

    row = lambda z: z.reshape(1, c)
    consts = [mu, w_rkv[0], w_rkv[1], w_rkv[2], row(w0), w1, w2, row(a0), a1, a2, g1, g2]
    args = [h, h_prev] + consts
    in_specs = [tok, tok] + [whole(z) for z in consts]
    if vres is not None:
        v0, v1, v2 = vres
        extra = [row(v0), v1, v2]
        args += [v_first] + extra
        in_specs += [tok] + [whole(z) for z in extra]
    return pl.pallas_call(
        functools.partial(_rwkv_proj_body, has_vres=vres is not None),
        grid=(m // tm,),
        in_specs=in_specs,
        out_specs=[tok] * 6,
        out_shape=[jax.ShapeDtypeStruct((m, c), F32)] * 6,
        compiler_params=_cparams("parallel"),
        name="rwkv_proj",
    )(*args)


def _to_lanes(z, b, t):
    return jnp.transpose(z.reshape(b, t, RWKV_HEADS, RWKV_HEAD), (1, 3, 0, 2)).reshape(t, RWKV_HEAD, b * RWKV_HEADS)


def _head_param(p, b):
    ph = p.reshape(RWKV_HEADS, RWKV_HEAD).T
    return jnp.tile(ph, (1, b))


def _rwkv_layer(x, b, t, shift_prev, s0, v_first, vres, norm_g, mu, w_rkv, w0, w1, w2, a0, a1, a2,
                g1, g2, k_k, k_a, r_k, ln_g, ln_b, w_o):
    m, c = x.shape
    h = _rmsnorm(x, norm_g)
    h3 = h.reshape(b, t, c)
    h_prev = jnp.concatenate([shift_prev[:, None, :], h3[:, :-1]], axis=1)
    r, k, v, wl, a, g = _rwkv_proj(h, h_prev.reshape(m, c), mu, w_rkv, w0, w1, w2, a0, a1, a2, g1, g2,
                                   v_first, vres)
    if vres is None:
        v_first = v
    tl = lambda z: _to_lanes(z, b, t)
    hp = lambda p: _head_param(p, b)
    s0_l = jnp.transpose(s0, (3, 2, 0, 1)).reshape(RWKV_HEAD, RWKV_HEAD, b * RWKV_HEADS)
    y_l, s_l = _wkv(tl(r), tl(wl), tl(k), tl(v), tl(a), hp(k_k), hp(k_a), hp(r_k), hp(ln_g), hp(ln_b),
                    s0_l, 32)
    y = jnp.transpose(y_l.reshape(t, RWKV_HEAD, b, RWKV_HEADS), (2, 0, 3, 1)).reshape(m, c)
    s_new = jnp.transpose(s_l.reshape(RWKV_HEAD, RWKV_HEAD, b, RWKV_HEADS), (2, 3, 1, 0))
    out = _matmul(y, w_o, mul=g, res=x)
    return out, h3[:, -1], s_new, v_first


def _conv_tail(c, ln_g, ln_b):
    mu = jnp.mean(c, axis=-1, keepdims=True)
    xc = c - mu
    var = jnp.mean(xc * xc, axis=-1, keepdims=True)
    return _silu(xc * lax.rsqrt(var + LN_EPS) * ln_g + ln_b)


def _conv_prompt_body(x_ref, buf_ref, g_ref, wa_ref, wb_ref, ba_ref, bb_ref, wdw_ref, bdw_ref,
                      lg_ref, lb_ref, w2_ref, b2_ref, o_ref, tail_ref, ext_ref):
    ti = pl.program_id(1)
    tm = x_ref.shape[0]

    @pl.when(ti == 0)
    def _():
        ext_ref[pl.ds(0, CONV_HALO), :] = buf_ref[...]

    x = x_ref[...]
    h = _rms(x, g_ref[...]).astype(BF16)
    ua = jnp.dot(h, wa_ref[...], preferred_element_type=F32) + ba_ref[...]
    ub = jnp.dot(h, wb_ref[...], preferred_element_type=F32) + bb_ref[...]
    ext_ref[pl.ds(CONV_HALO, tm), :] = ua * _sigmoid(ub)
    c = jnp.zeros_like(x) + bdw_ref[...]
    for j in range(CONV_WIDTH):
        c = c + ext_ref[pl.ds(CONV_HALO - CONV_BUF + j, tm), :] * wdw_ref[pl.ds(j, 1), :]
    z = _conv_tail(c, lg_ref[...], lb_ref[...]).astype(BF16)
    o_ref[...] = x + jnp.dot(z, w2_ref[...], preferred_element_type=F32) + b2_ref[...]
    tail = ext_ref[pl.ds(tm, CONV_HALO), :]
    ext_ref[pl.ds(0, CONV_HALO), :] = tail

    @pl.when(ti == pl.num_programs(1) - 1)
    def _():
        tail_ref[...] = tail


def _conv_prompt(x, buf, norm_g, w_pw1, b_pw1, w_dw, b_dw, ln_g, ln_b, w_pw2, b_pw2, tm):
    b, t, d = x.shape
    bufp = jnp.pad(buf, ((0, 0), (CONV_HALO - CONV_BUF, 0), (0, 0)))
    row = lambda z: z.reshape(1, -1)
    vec = pl.BlockSpec((1, d), lambda i, j: (0, 0))
    out, tail = pl.pallas_call(
        _conv_prompt_body,
        grid=(b, t // tm),
        in_specs=[
            pl.BlockSpec((None, tm, d), lambda i, j: (i, j, 0)),
            pl.BlockSpec((None, CONV_HALO, d), lambda i, j: (i, 0, 0)),
            vec,
            pl.BlockSpec((d, d), lambda i, j: (0, 0)),
            pl.BlockSpec((d, d), lambda i, j: (0, 1)),
            pl.BlockSpec((1, d), lambda i, j: (0, 0)),
            pl.BlockSpec((1, d), lambda i, j: (0, 1)),
            pl.BlockSpec((CONV_WIDTH, d), lambda i, j: (0, 0)),
            vec, vec, vec,
            pl.BlockSpec((d, d), lambda i, j: (0, 0)),
            vec,
        ],
        out_specs=[pl.BlockSpec((None, tm, d), lambda i, j: (i, j, 0)),
                   pl.BlockSpec((None, CONV_HALO, d), lambda i, j: (i, 0, 0))],
        out_shape=[jax.ShapeDtypeStruct((b, t, d), F32), jax.ShapeDtypeStruct((b, CONV_HALO, d), F32)],
        scratch_shapes=[pltpu.VMEM((CONV_HALO + tm, d), F32)],
        compiler_params=_cparams("parallel", "arbitrary"),
        name="conv_prompt",
    )(x, bufp, row(norm_g), w_pw1, w_pw1, row(b_pw1), row(b_pw1), w_dw, row(b_dw), row(ln_g), row(ln_b),
      w_pw2, row(b_pw2))
    return out, tail[:, CONV_HALO - CONV_BUF:, :]


def _conv_sample_body(x_ref, ext_ref, wdw_ref, bdw_ref, lg_ref, lb_ref, w2_ref, b2_ref, o_ref):
    s = x_ref.shape[0]
    for t in range(s):
        c = jnp.zeros(x_ref.shape[1:], F32) + bdw_ref[...]
        for j in range(CONV_WIDTH):
            c = c + ext_ref[t + j] * wdw_ref[pl.ds(j, 1), :]
        z = _conv_tail(c, lg_ref[...], lb_ref[...]).astype(BF16)
        o_ref[t] = x_ref[t] + jnp.dot(z, w2_ref[...], preferred_element_type=F32) + b2_ref[...]


def _conv_sample(x, buf, norm_g, w_pw1, b_pw1, w_dw, b_dw, ln_g, ln_b, w_pw2, b_pw2, bb):
    b, s, d = x.shape
    u = _matmul(x.reshape(b * s, d), w_pw1, norm_g=norm_g, bias=b_pw1, glu=True).reshape(b, s, d)
    ext = jnp.concatenate([buf, u], axis=1)
    ext_t = jnp.transpose(ext, (1, 0, 2))
    x_t = jnp.transpose(x, (1, 0, 2))
    row = lambda z: z.reshape(1, -1)
    vec = pl.BlockSpec((1, d), lambda i: (0, 0))
    out_t = pl.pallas_call(
        _conv_sample_body,
        grid=(b // bb,),
        in_specs=[
            pl.BlockSpec((s, bb, d), lambda i: (0, i, 0)),
            pl.BlockSpec((CONV_BUF + s, bb, d), lambda i: (0, i, 0)),
            pl.BlockSpec((CONV_WIDTH, d), lambda i: (0, 0)),
            vec, vec, vec,
            pl.BlockSpec((d, d), lambda i: (0, 0)),
            vec,
        ],
        out_specs=pl.BlockSpec((s, bb, d), lambda i: (0, i, 0)),
        out_shape=jax.ShapeDtypeStruct((s, b, d), F32),
        compiler_params=_cparams("parallel"),
        name="conv_sample",
    )(x_t, ext_t, w_dw, row(b_dw), row(ln_g), row(ln_b), w_pw2, row(b_pw2))
    return jnp.transpose(out_t, (1, 0, 2)), ext[:, -CONV_BUF:, :]


def _lambda_value(lam_ref, lam_init):
    lp = lam_ref[...]
    l1 = jnp.sum(lp[0:1, :] * lp[1:2, :], axis=-1, keepdims=True)
    l2 = jnp.sum(lp[2:3, :] * lp[3:4, :], axis=-1, keepdims=True)
    return jnp.exp(l1) - jnp.exp(l2) + lam_init


def _flash_body(q_ref, k_ref, v_ref, lam_ref, sub_ref, o_ref, *, lam_init, tq):
    qi = pl.program_id(2)
    q = q_ref[...] * ATT_SCALE
    lane = lax.broadcasted_iota(jnp.int32, q.shape, 1)
    zero = jnp.zeros_like(q)
    qs = jnp.concatenate([jnp.where(lane < ATT_DH, q, zero), jnp.where(lane >= ATT_DH, q, zero)], axis=0)
    nt = (((1,), (1,)), ((), ()))

    def chunk(start, width, carry, masked):
        m, l, acc = carry
        kc = k_ref[pl.ds(pl.multiple_of(start, tq), width), :]
        vc = v_ref[pl.ds(pl.multiple_of(start, tq), width), :]
        s = lax.dot_general(qs, kc, nt, preferred_element_type=F32)
        if masked:
            row = lax.broadcasted_iota(jnp.int32, s.shape, 0) % tq
            col = lax.broadcasted_iota(jnp.int32, s.shape, 1)
            s = jnp.where(col <= row, s, NEG_INF)
        m_new = jnp.maximum(m, jnp.max(s, axis=-1, keepdims=True))
        alpha = jnp.exp(m - m_new)
        p = jnp.exp(s - m_new)
        l = l * alpha + jnp.sum(p, axis=-1, keepdims=True)
        acc = acc * alpha + jnp.dot(p.astype(BF16), vc, preferred_element_type=F32)
        return m_new, l, acc

    init = (jnp.full((2 * tq, 1), NEG_INF, F32), jnp.zeros((2 * tq, 1), F32), jnp.zeros((2 * tq, ATT_VD), F32))
    carry = lax.fori_loop(0, qi // 2, lambda c, cr: chunk(c * 2 * tq, 2 * tq, cr, False), init)
    carry = lax.cond(qi % 2 == 1, lambda cr: chunk((qi - 1) * tq, tq, cr, False), lambda cr: cr, carry)
    _, l, acc = chunk(qi * tq, tq, carry, True)
    lam = _lambda_value(lam_ref, lam_init)
    o = acc / l
    o = o[:tq] - lam * o[tq:]
    o_ref[...] = _rms(o, sub_ref[...], SUBLN_EPS) * (1.0 - lam_init)


def _flash_prompt(q, k, v, lam_p, subln, lam_init, tq):
    b, t, d = q.shape
    kv_spec = pl.BlockSpec((None, t, ATT_VD), lambda i, h, j: (i, 0, h))
    return pl.pallas_call(
        functools.partial(_flash_body, lam_init=lam_init, tq=tq),
        grid=(b, ATT_HEADS, t // tq),
        in_specs=[
            pl.BlockSpec((None, tq, ATT_VD), lambda i, h, j: (i, j, h)),
            kv_spec,
            kv_spec,
            pl.BlockSpec((4, ATT_DH), lambda i, h, j: (0, 0)),
            pl.BlockSpec((1, ATT_VD), lambda i, h, j: (0, 0)),
        ],
        out_specs=pl.BlockSpec((None, tq, ATT_VD), lambda i, h, j: (i, j, h)),
        out_shape=jax.ShapeDtypeStruct((b, t, d), F32),
        compiler_params=_cparams("parallel", "parallel", "arbitrary"),
        name="flash_prompt",
    )(q, k, v, lam_p, subln.reshape(1, ATT_VD))


def _paged_body(*refs, lam_init, pages_per_step):
    pp = pages_per_step
    pt_ref = refs[0]
    q_ref, kn_ref, vn_ref = refs[1:4]
    kp_refs = refs[4:4 + pp]
    vp_refs = refs[4 + pp:4 + 2 * pp]
    lam_ref, sub_ref, o_ref, qm_ref, m_ref, l_ref, acc_ref = refs[4 + 2 * pp:]
    del pt_ref
    step = pl.program_id(1)
    s_len = o_ref.shape[0]
    grp = q_ref.shape[0]
    nt = (((1,), (1,)), ((), ()))
    heads = range(ATT_HEADS)

    def head_cols(x, h):
        return x[:, h * ATT_VD:(h + 1) * ATT_VD]

    @pl.when(step == 0)
    def _():
        q = q_ref[...] * ATT_SCALE
        r = lax.broadcasted_iota(jnp.int32, q.shape, 0)
        c = lax.broadcasted_iota(jnp.int32, q.shape, 1)
        q = jnp.where((c % ATT_VD) // ATT_DH == r // s_len, q, 0.0).astype(BF16)
        qm_ref[...] = jnp.concatenate([head_cols(q, h) for h in heads], axis=0)
        m_ref[...] = jnp.full(m_ref.shape, NEG_INF, F32)
        l_ref[...] = jnp.zeros(l_ref.shape, F32)
        acc_ref[...] = jnp.zeros(acc_ref.shape, F32)

    qm = qm_ref[...]

    def head_rows(x, h):
        return x[h * grp:(h + 1) * grp]

    def merge(s, weighted_values):
        m = m_ref[...]
        m_new = jnp.maximum(m, jnp.max(s, axis=-1, keepdims=True))
        alpha = jnp.exp(m - m_new)
        p = jnp.exp(s - m_new)
        l_ref[...] = l_ref[...] * alpha + jnp.sum(p, axis=-1, keepdims=True)
        acc_ref[...] = acc_ref[...] * alpha + weighted_values(p.astype(BF16))
        m_ref[...] = m_new

    page_rows = PAGE_SIZE * ATT_HEADS
    own_head = (lax.broadcasted_iota(jnp.int32, (qm.shape[0], page_rows), 0) // grp
                == lax.broadcasted_iota(jnp.int32, (qm.shape[0], page_rows), 1) % ATT_HEADS)
    for kp_ref, vp_ref in zip(kp_refs, vp_refs):
        ka = kp_ref[...].reshape(page_rows, ATT_VD).astype(BF16)
        va = vp_ref[...].reshape(page_rows, ATT_VD).astype(BF16)
        s = lax.dot_general(qm, ka, nt, preferred_element_type=F32)
        merge(jnp.where(own_head, s, NEG_INF), lambda pb: jnp.dot(pb, va, preferred_element_type=F32))

    @pl.when(step == pl.num_programs(1) - 1)
    def _():
        kn = kn_ref[...].astype(BF16)
        vn = vn_ref[...].astype(BF16)
        s = jnp.concatenate([lax.dot_general(head_rows(qm, h), head_cols(kn, h), nt, preferred_element_type=F32)
                             for h in heads], axis=0)
        t = lax.broadcasted_iota(jnp.int32, s.shape, 0) % s_len
        c = lax.broadcasted_iota(jnp.int32, s.shape, 1)
        s = jnp.where(c <= t, s, NEG_INF)
        merge(s, lambda pb: jnp.concatenate(
            [jnp.dot(head_rows(pb, h), head_cols(vn, h), preferred_element_type=F32) for h in heads], axis=0))
        o = acc_ref[...] / l_ref[...]
        lam = _lambda_value(lam_ref, lam_init)
        for h in heads:
            blk = head_rows(o, h)
            oh = blk[:s_len] - lam * blk[s_len:]
            o_ref[:, h * ATT_VD:(h + 1) * ATT_VD] = _rms(oh, sub_ref[...], SUBLN_EPS) * (1.0 - lam_init)


def _paged_sample(q, k_new, v_new, cache_k, cache_v, layer, page_table, lam_p, subln, lam_init, pages_per_step):
    b, s, d = q.shape
    n_pages = page_table.shape[1]
    pp = pages_per_step
    grp = 2 * s
    rows = ATT_HEADS * grp
    q2 = jnp.concatenate([q, q], axis=1)
    k_new, v_new = (jnp.pad(z, ((0, 0), (0, grp - s), (0, 0))) for z in (k_new, v_new))
    tok = pl.BlockSpec((None, grp, d), lambda i, j, pt: (i, 0, 0))

    def page_spec(o):
        return pl.BlockSpec((None, None, PAGE_SIZE, ATT_HEADS, ATT_VD),
                            lambda i, j, pt: (layer, pt[i, j * pp + o], 0, 0, 0))

    grid_spec = pltpu.PrefetchScalarGridSpec(
        num_scalar_prefetch=1,
        grid=(b, n_pages // pp),
        in_specs=[tok, tok, tok] + [page_spec(o) for o in range(pp)] + [page_spec(o) for o in range(pp)] + [
            pl.BlockSpec((4, ATT_DH), lambda i, j, pt: (0, 0)),
            pl.BlockSpec((1, ATT_VD), lambda i, j, pt: (0, 0)),
        ],
        out_specs=pl.BlockSpec((None, s, d), lambda i, j, pt: (i, 0, 0)),
        scratch_shapes=[pltpu.VMEM((rows, ATT_VD), BF16), pltpu.VMEM((rows, 1), F32), pltpu.VMEM((rows, 1), F32),
                        pltpu.VMEM((rows, ATT_VD), F32)],
    )
    return pl.pallas_call(
        functools.partial(_paged_body, lam_init=lam_init, pages_per_step=pp),
        grid_spec=grid_spec,
        out_shape=jax.ShapeDtypeStruct((b, s, d), F32),
        compiler_params=_cparams("parallel", "arbitrary"),
        name="paged_sample",
    )(page_table, q2, k_new, v_new, *([cache_k] * pp), *([cache_v] * pp), lam_p, subln.reshape(1, ATT_VD))


def kernel(x_prompt, x_sample, state_wkv, state_shift, state_conv, cache_k, cache_v, page_table, norm_mix, norm_ffn, norm_final, rwkv_mu, rwkv_w_rkv, rwkv_w0, rwkv_w1, rwkv_w2, rwkv_a0, rwkv_a1, rwkv_a2, rwkv_v0, rwkv_v1, rwkv_v2, rwkv_g1, rwkv_g2, rwkv_k_k, rwkv_k_a, rwkv_r_k, rwkv_ln_g, rwkv_ln_b, rwkv_w_o, conv_w_pw1, conv_b_pw1, conv_w_dw, conv_b_dw, conv_ln_g, conv_ln_b, conv_w_pw2, conv_b_pw2, attn_w_qkv, attn_lambda, attn_subln, attn_w_o, ffn_w13, ffn_w2, moe_router, moe_w13, moe_w2):
    d = D_MODEL
    bf = lambda z: z.astype(BF16)
    w_rkv, w1, w2, a1, a2, v1, v2, g1, g2, w_o = map(
        bf, (rwkv_w_rkv, rwkv_w1, rwkv_w2, rwkv_a1, rwkv_a2, rwkv_v1, rwkv_v2, rwkv_g1, rwkv_g2, rwkv_w_o))
    c_pw1, c_pw2 = bf(conv_w_pw1), bf(conv_w_pw2)
    a_qkv, a_wo = bf(attn_w_qkv), bf(attn_w_o)
    f_w13, f_w2 = bf(ffn_w13), bf(ffn_w2)
    m_w13, m_w2 = bf(moe_w13), bf(moe_w2)
    d_ff = ffn_w2.shape[1]
    n_pool = cache_k.shape[1]

    def run(x3, wkv_in, shift_in, conv_in, sample):
        b, t, _ = x3.shape
        m = b * t
        x = x3.reshape(m, d)
        wkv_out, shift_out, conv_out, k_out, v_out = [], [], [], [], []
        v_first = None
        for i in range(DEPTH):
            kind, j = i % 3, i // 3
            if kind == 0:
                vres = None if j == 0 else (rwkv_v0[j - 1], v1[j - 1], v2[j - 1])
                x, sh, s_new, v_first = _rwkv_layer(
                    x, b, t, shift_in[j], wkv_in[j], v_first, vres, norm_mix[i], rwkv_mu[j], w_rkv[j],
                    rwkv_w0[j], w1[j], w2[j], rwkv_a0[j], a1[j], a2[j], g1[j], g2[j], rwkv_k_k[j],
                    rwkv_k_a[j], rwkv_r_k[j], rwkv_ln_g[j], rwkv_ln_b[j], w_o[j])
                wkv_out.append(s_new)
                shift_out.append(sh)
            elif kind == 1:
                conv_args = (norm_mix[i], c_pw1[j], conv_b_pw1[j], conv_w_dw[j], conv_b_dw[j], conv_ln_g[j],
                             conv_ln_b[j], c_pw2[j], conv_b_pw2[j])
                if sample:
                    y3, buf = _conv_sample(x.reshape(b, t, d), conv_in[j], *conv_args, 32)
                else:
                    y3, buf = _conv_prompt(x.reshape(b, t, d), conv_in[j], *conv_args, 256)
                x = y3.reshape(m, d)
                conv_out.append(buf)
            else:
                lam_init = 0.8 - 0.6 * math.exp(-0.3 * i)
                q, k, v, q_b, k_b, v_b = _qkv_proj(x, norm_mix[i], a_qkv[j])
                seq = lambda z: z.reshape(b, t, d)
                if sample:
                    o = _paged_sample(seq(q), seq(k), seq(v), cache_k, cache_v, j, page_table, attn_lambda[j],
                                      attn_subln[j], lam_init, 4)
                else:
                    o = _flash_prompt(seq(q_b), seq(k_b), seq(v_b), attn_lambda[j], attn_subln[j], lam_init, 256)
                x = _matmul(o.reshape(m, d), a_wo[j], res=x)
                k_out.append(k.reshape(b, t, ATT_HEADS, 2 * ATT_DH))
                v_out.append(v.reshape(b, t, ATT_HEADS, ATT_VD))
            if i % 2 == 0:
                x = _ffn_dense(x, norm_ffn[i], f_w13[i // 2], f_w2[i // 2], d_ff // 2)
            else:
                x = _moe(x, norm_ffn[i], moe_router[i // 2], m_w13[i // 2], m_w2[i // 2], 512)
        y = _rmsnorm(x, norm_final).reshape(b, t, d)
        return (y, jnp.stack(wkv_out), jnp.stack(shift_out), jnp.stack(conv_out), jnp.stack(k_out),
                jnp.stack(v_out))

    bp = x_prompt.shape[0]
    wkv0 = jnp.zeros((state_wkv.shape[0], bp) + state_wkv.shape[2:], state_wkv.dtype)
    shift0 = jnp.zeros((state_shift.shape[0], bp, d), x_prompt.dtype)
    conv0 = jnp.zeros((state_conv.shape[0], bp, CONV_BUF, d), x_prompt.dtype)
    y_p, wkv_p, sh_p, conv_p, k_p, v_p = run(x_prompt, wkv0, shift0, conv0, False)
    y_s, wkv_s, sh_s, conv_s, k_s, v_s = run(x_sample, state_wkv, state_shift, state_conv, True)
    return (y_p, y_s, wkv_p, wkv_s, sh_p, sh_s, conv_p, conv_s, k_p, k_s, v_p, v_s)
```

```python
import functools
import math

import jax
import jax.numpy as jnp
from jax import lax
from jax.experimental import pallas as pl
from jax.experimental.pallas import tpu as pltpu

F32 = jnp.float32
BF16 = jnp.bfloat16

D_MODEL = 1024
DEPTH = 4
PAGE_SIZE = 128
RWKV_HEAD = 64
RWKV_HEADS = D_MODEL // RWKV_HEAD
GN_EPS = 64e-5
CONV_WIDTH = 31
CONV_BUF = CONV_WIDTH - 1
LN_EPS = 1e-5
ATT_HEADS = 8
ATT_DH = 64
ATT_VD = 128
ATT_SCALE = ATT_DH ** -0.5
SUBLN_EPS = 1e-5
N_EXPERTS = 8
TOP_K = 2
RMS_EPS = 1e-6

V7X_LANES = 128
V7X_SUBLANES = 8
V7X_VMEM_LIMIT_BYTES = 56 * 1024 * 1024

ROW_TILE = 512
EXPERT_TILE_MIN = 128
EXPERT_TILE_MAX = 1024
CONV_HALO = 32
NEG_INF = float("-inf")


def _cparams(*sem):
    return pltpu.CompilerParams(dimension_semantics=sem, vmem_limit_bytes=V7X_VMEM_LIMIT_BYTES)


def _rms(x, g, eps=RMS_EPS):
    return x * lax.rsqrt(jnp.mean(x * x, axis=-1, keepdims=True) + eps) * g


def _sigmoid(x):
    return 1.0 / (1.0 + jnp.exp(-x))


def _silu(x):
    return x * _sigmoid(x)


def _row_tile(m):
    return min(ROW_TILE, m)


def _rms_body(x_ref, g_ref, o_ref):
    o_ref[...] = _rms(x_ref[...], g_ref[...]).astype(o_ref.dtype)


def _rmsnorm(x, g, out_dtype=F32):
    m, d = x.shape
    tm = _row_tile(m)
    return pl.pallas_call(
        _rms_body,
        grid=(m // tm,),
        in_specs=[pl.BlockSpec((tm, d), lambda i: (i, 0)), pl.BlockSpec((1, d), lambda i: (0, 0))],
        out_specs=pl.BlockSpec((tm, d), lambda i: (i, 0)),
        out_shape=jax.ShapeDtypeStruct((m, d), out_dtype),
        compiler_params=_cparams("parallel"),
        name="rmsnorm",
    )(x, g.reshape(1, d))


def _mm_body(*refs, norm, mul, bias, res, glu):
    it = iter(refs)
    x_ref = next(it)
    g_ref = next(it) if norm else None
    m_ref = next(it) if mul else None
    w_ref = next(it)
    w2_ref = next(it) if glu else None
    b_ref = next(it) if bias else None
    b2_ref = next(it) if (bias and glu) else None
    r_ref = next(it) if res else None
    o_ref = next(it)
    xs_ref = next(it)

    @pl.when(pl.program_id(1) == 0)
    def _():
        x = x_ref[...].astype(F32)
        if norm:
            x = _rms(x, g_ref[...])
        if mul:
            x = x * m_ref[...]
        xs_ref[...] = x.astype(BF16)

    xs = xs_ref[...]
    y = jnp.dot(xs, w_ref[...], preferred_element_type=F32)
    if bias:
        y = y + b_ref[...]
    if glu:
        y2 = jnp.dot(xs, w2_ref[...], preferred_element_type=F32)
        if bias:
            y2 = y2 + b2_ref[...]
        y = y * _sigmoid(y2)
    if res:
        y = y + r_ref[...]
    o_ref[...] = y.astype(o_ref.dtype)


def _matmul(x, w, *, norm_g=None, mul=None, bias=None, res=None, glu=False):
    m, k = x.shape
    n = w.shape[1]
    n_out = n // 2 if glu else n
    tm = _row_tile(m)
    tn = n_out
    nj = n_out // tn
    in_specs = [pl.BlockSpec((tm, k), lambda i, j: (i, 0))]
    args = [x]
    if norm_g is not None:
        in_specs.append(pl.BlockSpec((1, k), lambda i, j: (0, 0)))
        args.append(norm_g.reshape(1, k))
    if mul is not None:
        in_specs.append(pl.BlockSpec((tm, k), lambda i, j: (i, 0)))
        args.append(mul)
    in_specs.append(pl.BlockSpec((k, tn), lambda i, j: (0, j)))
    args.append(w)
    if glu:
        in_specs.append(pl.BlockSpec((k, tn), lambda i, j: (0, j + nj)))
        args.append(w)
    if bias is not None:
        b2d = bias.reshape(1, n)
        in_specs.append(pl.BlockSpec((1, tn), lambda i, j: (0, j)))
        args.append(b2d)
        if glu:
            in_specs.append(pl.BlockSpec((1, tn), lambda i, j: (0, j + nj)))
            args.append(b2d)
    if res is not None:
        in_specs.append(pl.BlockSpec((tm, tn), lambda i, j: (i, j)))
        args.append(res)
    body = functools.partial(_mm_body, norm=norm_g is not None, mul=mul is not None,
                             bias=bias is not None, res=res is not None, glu=glu)
    return pl.pallas_call(
        body,
        grid=(m // tm, nj),
        in_specs=in_specs,
        out_specs=pl.BlockSpec((tm, tn), lambda i, j: (i, j)),
        out_shape=jax.ShapeDtypeStruct((m, n_out), F32),
        scratch_shapes=[pltpu.VMEM((tm, k), BF16)],
        compiler_params=_cparams("parallel", "arbitrary"),
        name="matmul",
    )(*args)


def _qkv_body(x_ref, g_ref, w_ref, q_ref, k_ref, v_ref, qb_ref, kb_ref, vb_ref):
    d = q_ref.shape[1]
    xs = _rms(x_ref[...], g_ref[...]).astype(BF16)
    y = jnp.dot(xs, w_ref[...], preferred_element_type=F32)
    for n, (o_ref, ob_ref) in enumerate(((q_ref, qb_ref), (k_ref, kb_ref), (v_ref, vb_ref))):
        part = y[:, n * d:(n + 1) * d]
        o_ref[...] = part
        ob_ref[...] = part.astype(BF16)


def _qkv_proj(x, g, w):
    m, d = x.shape
    tm = min(256, m)
    tok = pl.BlockSpec((tm, d), lambda i: (i, 0))
    return pl.pallas_call(
        _qkv_body,
        grid=(m // tm,),
        in_specs=[tok, pl.BlockSpec((1, d), lambda i: (0, 0)), pl.BlockSpec(w.shape, lambda i: (0, 0))],
        out_specs=[tok] * 6,
        out_shape=[jax.ShapeDtypeStruct((m, d), F32)] * 3 + [jax.ShapeDtypeStruct((m, d), BF16)] * 3,
        compiler_params=_cparams("parallel"),
        name="qkv_proj",
    )(x, g.reshape(1, d), w)


def _rows_from_tiles(ref):
    rows, sub, lanes = ref.shape
    x = jnp.swapaxes(ref[...].reshape(rows // sub, sub, sub, lanes), 1, 2)
    return jnp.concatenate([x[:, s].reshape(rows, lanes) for s in range(sub)], axis=-1)


def _rows_to_tiles(ref, val):
    rows, sub, lanes = ref.shape
    x = jnp.stack([val[:, s * lanes:(s + 1) * lanes].reshape(rows // sub, sub, lanes) for s in range(sub)],
                  axis=1)
    ref[...] = jnp.swapaxes(x, 1, 2).reshape(rows, sub, lanes)


def _ffn_body(*refs, moe):
    if moe:
        te_ref, tv_ref, x_ref, g_ref, wg_ref, wu_ref, wd_ref, o_ref, hs_ref, acc_ref = refs
    else:
        x_ref, g_ref, wg_ref, wu_ref, wd_ref, o_ref, hs_ref, acc_ref = refs
    j = pl.program_id(1)
    nj = pl.num_programs(1)

    def compute():
        @pl.when(j == 0)
        def _():
            x = _rows_from_tiles(x_ref) if moe else x_ref[...]
            hs_ref[...] = _rms(x, g_ref[...]).astype(BF16)
            acc_ref[...] = jnp.zeros_like(acc_ref)

        h = hs_ref[...]
        gate = jnp.dot(h, wg_ref[...], preferred_element_type=F32)
        up = jnp.dot(h, wu_ref[...], preferred_element_type=F32)
        a = (_silu(gate) * up).astype(BF16)
        acc_ref[...] += jnp.dot(a, wd_ref[...], preferred_element_type=F32)

        @pl.when(j == nj - 1)
        def _():
            if moe:
                _rows_to_tiles(o_ref, acc_ref[...])
            else:
                o_ref[...] = x_ref[...] + acc_ref[...]

    if moe:
        valid = tv_ref[pl.program_id(0)] > 0
        pl.when(valid)(compute)

        @pl.when(jnp.logical_and(jnp.logical_not(valid), j == nj - 1))
        def _():
            o_ref[...] = jnp.zeros_like(o_ref)
    else:
        compute()


def _ffn_dense(x, g, w13, w2, ff_tile):
    m, d = x.shape
    f = w2.shape[0]
    tm = _row_tile(m)
    nf = f // ff_tile
    return pl.pallas_call(
        functools.partial(_ffn_body, moe=False),
        grid=(m // tm, nf),
        in_specs=[
            pl.BlockSpec((tm, d), lambda i, j: (i, 0)),
            pl.BlockSpec((1, d), lambda i, j: (0, 0)),
            pl.BlockSpec((d, ff_tile), lambda i, j: (0, j)),
            pl.BlockSpec((d, ff_tile), lambda i, j: (0, j + nf)),
            pl.BlockSpec((ff_tile, d), lambda i, j: (j, 0)),
        ],
        out_specs=pl.BlockSpec((tm, d), lambda i, j: (i, 0)),
        out_shape=jax.ShapeDtypeStruct((m, d), F32),
        scratch_shapes=[pltpu.VMEM((tm, d), BF16), pltpu.VMEM((tm, d), F32)],
        compiler_params=_cparams("parallel", "arbitrary"),
        name="ffn_dense",
    )(x, g.reshape(1, d), w13, w13, w2)


def _ffn_experts(xs, g, w13, w2, tile_expert, tile_valid, tm, ff_tile):
    s, sub, lanes = xs.shape
    d = sub * lanes
    f = w2.shape[1]
    nf = f // ff_tile
    tok = pl.BlockSpec((tm, sub, lanes), lambda i, j, te, tv: (i, 0, 0))
    grid_spec = pltpu.PrefetchScalarGridSpec(
        num_scalar_prefetch=2,
        grid=(s // tm, nf),
        in_specs=[
            tok,
            pl.BlockSpec((1, d), lambda i, j, te, tv: (0, 0)),
            pl.BlockSpec((None, d, ff_tile), lambda i, j, te, tv: (te[i], 0, j)),
            pl.BlockSpec((None, d, ff_tile), lambda i, j, te, tv: (te[i], 0, j + nf)),
            pl.BlockSpec((None, ff_tile, d), lambda i, j, te, tv: (te[i], j, 0)),
        ],
        out_specs=tok,
        scratch_shapes=[pltpu.VMEM((tm, d), BF16), pltpu.VMEM((tm, d), F32)],
    )
    return pl.pallas_call(
        functools.partial(_ffn_body, moe=True),
        grid_spec=grid_spec,
        out_shape=jax.ShapeDtypeStruct((s, sub, lanes), F32),
        compiler_params=_cparams("parallel", "arbitrary"),
        name="ffn_experts",
    )(tile_expert, tile_valid, xs, g.reshape(1, d), w13, w13, w2)


def _router_body(x_ref, g_ref, whi_ref, wlo_ref, o_ref, g1_ref, g2_ref):
    h = _rms(x_ref[...], g_ref[...])
    hhi = h.astype(BF16)
    hlo = (h - hhi.astype(F32)).astype(BF16)
    whi = whi_ref[...]
    lg = (jnp.dot(hhi, whi, preferred_element_type=F32)
          + jnp.dot(hlo, whi, preferred_element_type=F32)
          + jnp.dot(hhi, wlo_ref[...], preferred_element_type=F32))
    lane = lax.broadcasted_iota(jnp.int32, lg.shape, 1)
    lg = jnp.where(lane < N_EXPERTS, lg, NEG_INF)
    m1 = jnp.max(lg, axis=-1, keepdims=True)
    i1 = jnp.min(jnp.where(lg == m1, lane, V7X_LANES), axis=-1, keepdims=True)
    lg2 = jnp.where(lane == i1, NEG_INF, lg)
    m2 = jnp.max(lg2, axis=-1, keepdims=True)
    i2 = jnp.min(jnp.where(lg2 == m2, lane, V7X_LANES), axis=-1, keepdims=True)
    e2 = jnp.exp(m2 - m1)
    g1 = 1.0 / (1.0 + e2)
    g2 = e2 / (1.0 + e2)
    out = jnp.where(lane == 0, i1.astype(F32),
                    jnp.where(lane == 1, i2.astype(F32),
                              jnp.where(lane == 2, g1, jnp.where(lane == 3, g2, 0.0))))
    o_ref[...] = out
    g1_ref[:, 0, :] = jnp.broadcast_to(g1, lg.shape)
    g2_ref[:, 0, :] = jnp.broadcast_to(g2, lg.shape)


def _router(x, g, w_router):
    m, d = x.shape
    tm = _row_tile(m)
    wpad = jnp.zeros((d, V7X_LANES), F32).at[:, :N_EXPERTS].set(w_router)
    whi = wpad.astype(BF16)
    wlo = (wpad - whi.astype(F32)).astype(BF16)
    gate_spec = pl.BlockSpec((tm, 1, V7X_LANES), lambda i: (i, 0, 0))
    gate_shape = jax.ShapeDtypeStruct((m, 1, V7X_LANES), F32)
    return pl.pallas_call(
        _router_body,
        grid=(m // tm,),
        in_specs=[
            pl.BlockSpec((tm, d), lambda i: (i, 0)),
            pl.BlockSpec((1, d), lambda i: (0, 0)),
            pl.BlockSpec((d, V7X_LANES), lambda i: (0, 0)),
            pl.BlockSpec((d, V7X_LANES), lambda i: (0, 0)),
        ],
        out_specs=[pl.BlockSpec((tm, V7X_LANES), lambda i: (i, 0)), gate_spec, gate_spec],
        out_shape=[jax.ShapeDtypeStruct((m, V7X_LANES), F32), gate_shape, gate_shape],
        compiler_params=_cparams("parallel"),
        name="router",
    )(x, g.reshape(1, d), whi, wlo)


DMA_UNROLL = 8


def _gather_body(src_ref, x_hbm, o_ref, sem):
    tg = o_ref.shape[0]
    base = pl.program_id(0) * tg

    def tile_copy(r):
        return pltpu.make_async_copy(x_hbm.at[pl.ds(src_ref[base + r], 1)], o_ref.at[pl.ds(r, 1)], sem)

    def start(i, c):
        for u in range(DMA_UNROLL):
            tile_copy(i * DMA_UNROLL + u).start(priority=u % 2)
        return c

    def wait(i, c):
        for u in range(DMA_UNROLL):
            tile_copy(i * DMA_UNROLL + u).wait()
        return c

    lax.fori_loop(0, tg // DMA_UNROLL, start, 0)
    lax.fori_loop(0, tg // DMA_UNROLL, wait, 0)


def _gather_tokens(x3, src, tg):
    s = src.shape[0]
    _, sub, lanes = x3.shape
    grid_spec = pltpu.PrefetchScalarGridSpec(
        num_scalar_prefetch=1,
        grid=(s // tg,),
        in_specs=[pl.BlockSpec(memory_space=pl.ANY)],
        out_specs=pl.BlockSpec((tg, sub, lanes), lambda i, src: (i, 0, 0)),
        scratch_shapes=[pltpu.SemaphoreType.DMA(())],
    )
    return pl.pallas_call(
        _gather_body,
        grid_spec=grid_spec,
        out_shape=jax.ShapeDtypeStruct((s, sub, lanes), x3.dtype),
        compiler_params=_cparams("arbitrary"),
        name="moe_gather",
    )(src, x3)


def _combine_body(dest_ref, x_ref, g1_ref, g2_ref, y_hbm, o_ref, buf, sem):
    tc = o_ref.shape[0]
    base = pl.program_id(0) * tc

    def tile_copy(r, k):
        return pltpu.make_async_copy(y_hbm.at[pl.ds(dest_ref[TOP_K * (base + r) + k], 1)],
                                     buf.at[k, pl.ds(r, 1)], sem)

    def start(i, c):
        for u in range(DMA_UNROLL):
            for k in range(TOP_K):
                tile_copy(i * DMA_UNROLL + u, k).start(priority=k)
        return c

    def wait(i, c):
        for u in range(DMA_UNROLL):
            for k in range(TOP_K):
                tile_copy(i * DMA_UNROLL + u, k).wait()
        return c

    lax.fori_loop(0, tc // DMA_UNROLL, start, 0)
    lax.fori_loop(0, tc // DMA_UNROLL, wait, 0)
    o_ref[...] = x_ref[...] + g1_ref[...] * buf[0] + g2_ref[...] * buf[1]


def _moe_combine(x3, g1, g2, y3, dest, tc):
    m, sub, lanes = x3.shape
    tok = pl.BlockSpec((tc, sub, lanes), lambda i, dest: (i, 0, 0))
    gate = pl.BlockSpec((tc, 1, lanes), lambda i, dest: (i, 0, 0))
    grid_spec = pltpu.PrefetchScalarGridSpec(
        num_scalar_prefetch=1,
        grid=(m // tc,),
        in_specs=[tok, gate, gate, pl.BlockSpec(memory_space=pl.ANY)],
        out_specs=tok,
        scratch_shapes=[pltpu.VMEM((TOP_K, tc, sub, lanes), F32), pltpu.SemaphoreType.DMA(())],
    )
    return pl.pallas_call(
        _combine_body,
        grid_spec=grid_spec,
        out_shape=jax.ShapeDtypeStruct((m, sub, lanes), F32),
        compiler_params=_cparams("arbitrary"),
        name="moe_combine",
    )(dest, x3, g1, g2, y3)


def _moe(x, g, w_router, w13, w2, ff_tile):
    m, d = x.shape
    tm = max(EXPERT_TILE_MIN, min(EXPERT_TILE_MAX, m // 16))
    r, g1, g2 = _router(x, g, w_router)
    e_flat = r[:, :TOP_K].astype(jnp.int32).reshape(-1)
    n_slots = m * TOP_K
    n_tiles = n_slots // tm + N_EXPERTS
    s_pad = n_tiles * tm
    onehot = (e_flat[:, None] == jnp.arange(N_EXPERTS, dtype=jnp.int32)[None, :]).astype(jnp.int32)
    rank = jnp.sum((jnp.cumsum(onehot, axis=0) - 1) * onehot, axis=1)
    counts = jnp.sum(onehot, axis=0)
    padded = ((counts + tm - 1) // tm) * tm
    ends = jnp.cumsum(padded)
    starts = ends - padded
    dest = (jnp.sum(onehot * starts[None, :], axis=1) + rank).astype(jnp.int32)
    token = jnp.arange(n_slots, dtype=jnp.int32) // TOP_K
    src = (jnp.arange(s_pad, dtype=jnp.int32) % m).at[dest].set(token)
    tile_start = jnp.arange(n_tiles, dtype=jnp.int32) * tm
    tile_valid = (tile_start < ends[-1]).astype(jnp.int32)
    tile_expert = jnp.sum((tile_start[:, None] >= ends[None, :]).astype(jnp.int32), axis=1)
    last_expert = jnp.max(jnp.where(tile_valid > 0, tile_expert, 0))
    tile_expert = jnp.where(tile_valid > 0, tile_expert, last_expert).astype(jnp.int32)
    x3 = x.reshape(m, d // V7X_LANES, V7X_LANES)
    xs = _gather_tokens(x3, src, tm)
    ys = _ffn_experts(xs, g, w13, w2, tile_expert, tile_valid, tm, ff_tile)
    return _moe_combine(x3, g1, g2, ys, dest, min(256, m)).reshape(m, d)


def _wkv_body(r_ref, wl_ref, k_ref, v_ref, a_ref, kk_ref, ka_ref, rk_ref, lg_ref, lb_ref, s0_ref,
              y_ref, s_ref, scr):
    tc = pl.program_id(1)
    n = RWKV_HEAD
    half = n // 2
    tt = r_ref.shape[0]
    W1, B1, K1, R1, W2, B2, K2, R2, AN1, AN2 = range(10)

    @pl.when(tc == 0)
    def _():
        s_ref[...] = s0_ref[...]

    kkw = kk_ref[...]
    kaw = ka_ref[...]
    rkw = rk_ref[...]
    lgw = lg_ref[...]
    lbw = lb_ref[...]

    def unit_key(t):
        kk = k_ref[t] * kkw
        return kk / jnp.maximum(jnp.sqrt(jnp.sum(kk * kk, axis=0, keepdims=True)), 1e-12)

    def decay(t):
        return jnp.exp(-jnp.exp(wl_ref[t]))

    def row(slot, i, rows):
        return jnp.broadcast_to(scr[slot, pl.ds(i, 1), :], (rows, V7X_LANES))

    def set_next(t1, t2):
        scr[AN1] = -unit_key(t1)
        scr[AN2] = decay(t1) * -unit_key(t2)

    def emit(t, y, r, k2, v):
        ym = jnp.mean(y, axis=0, keepdims=True)
        yc = y - ym
        yn = yc * lax.rsqrt(jnp.mean(yc * yc, axis=0, keepdims=True) + GN_EPS)
        bonus = jnp.sum(r * k2 * rkw, axis=0, keepdims=True) * v
        y_ref[t] = yn * lgw + lbw + bonus

    set_next(0, 1)
    sa0 = jnp.zeros((n, V7X_LANES), F32)
    sb0 = jnp.zeros((n, V7X_LANES), F32)
    for i in range(n):
        s_i = s_ref[i]
        sa0 = sa0 + s_i * row(AN1, i, n)
        sb0 = sb0 + s_i * row(AN2, i, n)

    def pair(p, carry):
        sa1, sb = carry
        t1 = 2 * p
        t2 = t1 + 1
        r1, k1, v1, a1 = r_ref[t1], k_ref[t1], v_ref[t1], a_ref[t1]
        r2, k2, v2, a2 = r_ref[t2], k_ref[t2], v_ref[t2], a_ref[t2]
        kk1 = unit_key(t1)
        kk2 = unit_key(t2)
        k1m = k1 * (1.0 + (a1 - 1.0) * kaw)
        k2m = k2 * (1.0 + (a2 - 1.0) * kaw)
        b1 = kk1 * a1
        sa2 = (sb + sa1 * jnp.sum(b1 * -kk2, axis=0, keepdims=True)
               + v1 * jnp.sum(k1m * -kk2, axis=0, keepdims=True))
        scr[W1] = decay(t1)
        scr[B1] = b1
        scr[K1] = k1m
        scr[R1] = r1
        scr[W2] = decay(t2)
        scr[B2] = kk2 * a2
        scr[K2] = k2m
        scr[R2] = r2
        set_next(jnp.minimum(t1 + 2, tt - 1), jnp.minimum(t1 + 3, tt - 1))
        ys1, ys2, nas, nbs = [], [], [], []
        for vh in range(2):
            rows = slice(vh * half, (vh + 1) * half)
            sa1_h, sa2_h, v1_h, v2_h = sa1[rows], sa2[rows], v1[rows], v2[rows]

            def key_group(gi, acc):
                y1, y2, na, nb = acc
                base = pl.multiple_of(gi * V7X_SUBLANES, V7X_SUBLANES)
                for u in range(V7X_SUBLANES):
                    i = base + u
                    s1 = s_ref[i, rows, :] * row(W1, i, half) + sa1_h * row(B1, i, half) + v1_h * row(K1, i, half)
                    s2 = s1 * row(W2, i, half) + sa2_h * row(B2, i, half) + v2_h * row(K2, i, half)
                    s_ref[i, rows, :] = s2
                    y1 = y1 + s1 * row(R1, i, half)
                    y2 = y2 + s2 * row(R2, i, half)
                    na = na + s2 * row(AN1, i, half)
                    nb = nb + s2 * row(AN2, i, half)
                return y1, y2, na, nb

            zero = jnp.zeros((half, V7X_LANES), F32)
            y1, y2, na, nb = lax.fori_loop(0, n // V7X_SUBLANES, key_group, (zero, zero, zero, zero))
            ys1.append(y1)
            ys2.append(y2)
            nas.append(na)
            nbs.append(nb)
        emit(t1, jnp.concatenate(ys1, axis=0), r1, k1m, v1)
        emit(t2, jnp.concatenate(ys2, axis=0), r2, k2m, v2)
        return jnp.concatenate(nas, axis=0), jnp.concatenate(nbs, axis=0)

    lax.fori_loop(0, tt // 2, pair, (sa0, sb0))


def _wkv(r, wl, k, v, a, k_k, k_a, r_k, ln_g, ln_b, s0, time_tile):
    t, n, l = r.shape
    tt = min(time_tile, t)
    seq = pl.BlockSpec((tt, n, V7X_LANES), lambda i, j: (j, 0, i))
    par = pl.BlockSpec((n, V7X_LANES), lambda i, j: (0, i))
    st = pl.BlockSpec((n, n, V7X_LANES), lambda i, j: (0, 0, i))
    return pl.pallas_call(
        _wkv_body,
        grid=(l // V7X_LANES, t // tt),
        in_specs=[seq, seq, seq, seq, seq, par, par, par, par, par, st],
        out_specs=[seq, st],
        out_shape=[jax.ShapeDtypeStruct((t, n, l), F32), jax.ShapeDtypeStruct((n, n, l), F32)],
        scratch_shapes=[pltpu.VMEM((10, n, V7X_LANES), F32)],
        compiler_params=_cparams("parallel", "arbitrary"),
        name="wkv7",
    )(r, wl, k, v, a, k_k, k_a, r_k, ln_g, ln_b, s0)


def _softplus(z):
    return jnp.maximum(z, 0.0) + jnp.log1p(jnp.exp(-jnp.abs(z)))


def _rwkv_proj_body(*refs, has_vres):
    (h_ref, hp_ref, mu_ref, wr_ref, wk_ref, wv_ref, w0_ref, w1_ref, w2_ref,
     a0_ref, a1_ref, a2_ref, g1_ref, g2_ref) = refs[:14]
    n_in = 14
    if has_vres:
        vf_ref, v0_ref, v1_ref, v2_ref = refs[14:18]
        n_in = 18
    r_ref, k_ref, v_ref, wl_ref, a_ref, g_ref = refs[n_in:]
    h = h_ref[...]
    xx = hp_ref[...] - h

    def mix(i):
        return (h + xx * mu_ref[pl.ds(i, 1), :]).astype(BF16)

    def dot(x, w_ref):
        return jnp.dot(x.astype(BF16), w_ref[...], preferred_element_type=F32)

    r_ref[...] = dot(mix(0), wr_ref)
    z = w0_ref[...] + dot(jnp.tanh(dot(mix(1), w1_ref)), w2_ref)
    wl_ref[...] = -_softplus(-z) - 0.5
    k_ref[...] = dot(mix(2), wk_ref)
    xv = mix(3)
    v = dot(xv, wv_ref)
    if has_vres:
        v = v + (vf_ref[...] - v) * _sigmoid(v0_ref[...] + dot(dot(xv, v1_ref), v2_ref))
    v_ref[...] = v
    a_ref[...] = _sigmoid(a0_ref[...] + dot(dot(mix(4), a1_ref), a2_ref))
    g_ref[...] = dot(_sigmoid(dot(mix(5), g1_ref)), g2_ref)


def _rwkv_proj(h, h_prev, mu, w_rkv, w0, w1, w2, a0, a1, a2, g1, g2, v_first, vres):
    m, c = h.shape
    tm = min(256, m)
    tok = pl.BlockSpec((tm, c), lambda i: (i, 0))

    def whole(z):
        return pl.BlockSpec(z.shape, lambda i: (0,) * z.ndim)

    row = lambda z: z.reshape(1, c)
    consts = [mu, w_rkv[0], w_rkv[1], w_rkv[2], row(w0), w1, w2, row(a0), a1, a2, g1, g2]
    args = [h, h_prev] + consts
    in_specs = [tok, tok] + [whole(z) for z in consts]
    if vres is not None:
        v0, v1, v2 = vres
        extra = [row(v0), v1, v2]
        args += [v_first] + extra
        in_specs += [tok] + [whole(z) for z in extra]
    return pl.pallas_call(
        functools.partial(_rwkv_proj_body, has_vres=vres is not None),
        grid=(m // tm,),
        in_specs=in_specs,
        out_specs=[tok] * 6,
        out_shape=[jax.ShapeDtypeStruct((m, c), F32)] * 6,
        compiler_params=_cparams("parallel"),
        name="rwkv_proj",
    )(*args)


def _to_lanes(z, b, t):
    return jnp.transpose(z.reshape(b, t, RWKV_HEADS, RWKV_HEAD), (1, 3, 0, 2)).reshape(t, RWKV_HEAD, b * RWKV_HEADS)


def _head_param(p, b):
    ph = p.reshape(RWKV_HEADS, RWKV_HEAD).T
    return jnp.tile(ph, (1, b))


def _rwkv_layer(x, b, t, shift_prev, s0, v_first, vres, norm_g, mu, w_rkv, w0, w1, w2, a0, a1, a2,
                g1, g2, k_k, k_a, r_k, ln_g, ln_b, w_o):
    m, c = x.shape
    h = _rmsnorm(x, norm_g)
    h3 = h.reshape(b, t, c)
    h_prev = jnp.concatenate([shift_prev[:, None, :], h3[:, :-1]], axis=1)
    r, k, v, wl, a, g = _rwkv_proj(h, h_prev.reshape(m, c), mu, w_rkv, w0, w1, w2, a0, a1, a2, g1, g2,
                                   v_first, vres)
    if vres is None:
        v_first = v
    tl = lambda z: _to_lanes(z, b, t)
    hp = lambda p: _head_param(p, b)
    s0_l = jnp.transpose(s0, (3, 2, 0, 1)).reshape(RWKV_HEAD, RWKV_HEAD, b * RWKV_HEADS)
    y_l, s_l = _wkv(tl(r), tl(wl), tl(k), tl(v), tl(a), hp(k_k), hp(k_a), hp(r_k), hp(ln_g), hp(ln_b),
                    s0_l, 64)
    y = jnp.transpose(y_l.reshape(t, RWKV_HEAD, b, RWKV_HEADS), (2, 0, 3, 1)).reshape(m, c)
    s_new = jnp.transpose(s_l.reshape(RWKV_HEAD, RWKV_HEAD, b, RWKV_HEADS), (2, 3, 1, 0))
    out = _matmul(y, w_o, mul=g, res=x)
    return out, h3[:, -1], s_new, v_first


def _conv_tail(c, ln_g, ln_b):
    mu = jnp.mean(c, axis=-1, keepdims=True)
    xc = c - mu
    var = jnp.mean(xc * xc, axis=-1, keepdims=True)
    return _silu(xc * lax.rsqrt(var + LN_EPS) * ln_g + ln_b)


def _conv_prompt_body(x_ref, buf_ref, g_ref, wa_ref, wb_ref, ba_ref, bb_ref, wdw_ref, bdw_ref,
                      lg_ref, lb_ref, w2_ref, b2_ref, o_ref, tail_ref, ext_ref):
    ti = pl.program_id(1)
    tm = x_ref.shape[0]

    @pl.when(ti == 0)
    def _():
        ext_ref[pl.ds(0, CONV_HALO), :] = buf_ref[...]

    x = x_ref[...]
    h = _rms(x, g_ref[...]).astype(BF16)
    ua = jnp.dot(h, wa_ref[...], preferred_element_type=F32) + ba_ref[...]
    ub = jnp.dot(h, wb_ref[...], preferred_element_type=F32) + bb_ref[...]
    ext_ref[pl.ds(CONV_HALO, tm), :] = ua * _sigmoid(ub)
    c = jnp.zeros_like(x) + bdw_ref[...]
    for j in range(CONV_WIDTH):
        c = c + ext_ref[pl.ds(CONV_HALO - CONV_BUF + j, tm), :] * wdw_ref[pl.ds(j, 1), :]
    z = _conv_tail(c, lg_ref[...], lb_ref[...]).astype(BF16)
    o_ref[...] = x + jnp.dot(z, w2_ref[...], preferred_element_type=F32) + b2_ref[...]
    tail = ext_ref[pl.ds(tm, CONV_HALO), :]
    ext_ref[pl.ds(0, CONV_HALO), :] = tail

    @pl.when(ti == pl.num_programs(1) - 1)
    def _():
        tail_ref[...] = tail


def _conv_prompt(x, buf, norm_g, w_pw1, b_pw1, w_dw, b_dw, ln_g, ln_b, w_pw2, b_pw2, tm):
    b, t, d = x.shape
    bufp = jnp.pad(buf, ((0, 0), (CONV_HALO - CONV_BUF, 0), (0, 0)))
    row = lambda z: z.reshape(1, -1)
    vec = pl.BlockSpec((1, d), lambda i, j: (0, 0))
    out, tail = pl.pallas_call(
        _conv_prompt_body,
        grid=(b, t // tm),
        in_specs=[
            pl.BlockSpec((None, tm, d), lambda i, j: (i, j, 0)),
            pl.BlockSpec((None, CONV_HALO, d), lambda i, j: (i, 0, 0)),
            vec,
            pl.BlockSpec((d, d), lambda i, j: (0, 0)),
            pl.BlockSpec((d, d), lambda i, j: (0, 1)),
            pl.BlockSpec((1, d), lambda i, j: (0, 0)),
            pl.BlockSpec((1, d), lambda i, j: (0, 1)),
            pl.BlockSpec((CONV_WIDTH, d), lambda i, j: (0, 0)),
            vec, vec, vec,
            pl.BlockSpec((d, d), lambda i, j: (0, 0)),
            vec,
        ],
        out_specs=[pl.BlockSpec((None, tm, d), lambda i, j: (i, j, 0)),
                   pl.BlockSpec((None, CONV_HALO, d), lambda i, j: (i, 0, 0))],
        out_shape=[jax.ShapeDtypeStruct((b, t, d), F32), jax.ShapeDtypeStruct((b, CONV_HALO, d), F32)],
        scratch_shapes=[pltpu.VMEM((CONV_HALO + tm, d), F32)],
        compiler_params=_cparams("parallel", "arbitrary"),
        name="conv_prompt",
    )(x, bufp, row(norm_g), w_pw1, w_pw1, row(b_pw1), row(b_pw1), w_dw, row(b_dw), row(ln_g), row(ln_b),
      w_pw2, row(b_pw2))
    return out, tail[:, CONV_HALO - CONV_BUF:, :]


def _conv_sample_body(x_ref, ext_ref, wdw_ref, bdw_ref, lg_ref, lb_ref, w2_ref, b2_ref, o_ref):
    s = x_ref.shape[0]
    for t in range(s):
        c = jnp.zeros(x_ref.shape[1:], F32) + bdw_ref[...]
        for j in range(CONV_WIDTH):
            c = c + ext_ref[t + j] * wdw_ref[pl.ds(j, 1), :]
        z = _conv_tail(c, lg_ref[...], lb_ref[...]).astype(BF16)
        o_ref[t] = x_ref[t] + jnp.dot(z, w2_ref[...], preferred_element_type=F32) + b2_ref[...]


def _conv_sample(x, buf, norm_g, w_pw1, b_pw1, w_dw, b_dw, ln_g, ln_b, w_pw2, b_pw2, bb):
    b, s, d = x.shape
    u = _matmul(x.reshape(b * s, d), w_pw1, norm_g=norm_g, bias=b_pw1, glu=True).reshape(b, s, d)
    ext = jnp.concatenate([buf, u], axis=1)
    ext_t = jnp.transpose(ext, (1, 0, 2))
    x_t = jnp.transpose(x, (1, 0, 2))
    row = lambda z: z.reshape(1, -1)
    vec = pl.BlockSpec((1, d), lambda i: (0, 0))
    out_t = pl.pallas_call(
        _conv_sample_body,
        grid=(b // bb,),
        in_specs=[
            pl.BlockSpec((s, bb, d), lambda i: (0, i, 0)),
            pl.BlockSpec((CONV_BUF + s, bb, d), lambda i: (0, i, 0)),
            pl.BlockSpec((CONV_WIDTH, d), lambda i: (0, 0)),
            vec, vec, vec,
            pl.BlockSpec((d, d), lambda i: (0, 0)),
            vec,
        ],
        out_specs=pl.BlockSpec((s, bb, d), lambda i: (0, i, 0)),
        out_shape=jax.ShapeDtypeStruct((s, b, d), F32),
        compiler_params=_cparams("parallel"),
        name="conv_sample",
    )(x_t, ext_t, w_dw, row(b_dw), row(ln_g), row(ln_b), w_pw2, row(b_pw2))
    return jnp.transpose(out_t, (1, 0, 2)), ext[:, -CONV_BUF:, :]


def _lambda_value(lam_ref, lam_init):
    lp = lam_ref[...]
    l1 = jnp.sum(lp[0:1, :] * lp[1:2, :], axis=-1, keepdims=True)
    l2 = jnp.sum(lp[2:3, :] * lp[3:4, :], axis=-1, keepdims=True)
    return jnp.exp(l1) - jnp.exp(l2) + lam_init


def _flash_body(q_ref, k_ref, v_ref, lam_ref, sub_ref, o_ref, *, lam_init, tq):
    qi = pl.program_id(2)
    q = q_ref[...] * ATT_SCALE
    lane = lax.broadcasted_iota(jnp.int32, q.shape, 1)
    zero = jnp.zeros_like(q)
    qs = jnp.concatenate([jnp.where(lane < ATT_DH, q, zero), jnp.where(lane >= ATT_DH, q, zero)], axis=0)
    nt = (((1,), (1,)), ((), ()))

    def chunk(start, width, carry, masked):
        m, l, acc = carry
        kc = k_ref[pl.ds(pl.multiple_of(start, tq), width), :]
        vc = v_ref[pl.ds(pl.multiple_of(start, tq), width), :]
        s = lax.dot_general(qs, kc, nt, preferred_element_type=F32)
        if masked:
            row = lax.broadcasted_iota(jnp.int32, s.shape, 0) % tq
            col = lax.broadcasted_iota(jnp.int32, s.shape, 1)
            s = jnp.where(col <= row, s, NEG_INF)
        m_new = jnp.maximum(m, jnp.max(s, axis=-1, keepdims=True))
        alpha = jnp.exp(m - m_new)
        p = jnp.exp(s - m_new)
        l = l * alpha + jnp.sum(p, axis=-1, keepdims=True)
        acc = acc * alpha + jnp.dot(p.astype(BF16), vc, preferred_element_type=F32)
        return m_new, l, acc

    init = (jnp.full((2 * tq, 1), NEG_INF, F32), jnp.zeros((2 * tq, 1), F32), jnp.zeros((2 * tq, ATT_VD), F32))
    carry = lax.fori_loop(0, qi // 2, lambda c, cr: chunk(c * 2 * tq, 2 * tq, cr, False), init)
    carry = lax.cond(qi % 2 == 1, lambda cr: chunk((qi - 1) * tq, tq, cr, False), lambda cr: cr, carry)
    _, l, acc = chunk(qi * tq, tq, carry, True)
    lam = _lambda_value(lam_ref, lam_init)
    o = acc / l
    o = o[:tq] - lam * o[tq:]
    o_ref[...] = _rms(o, sub_ref[...], SUBLN_EPS) * (1.0 - lam_init)


def _flash_prompt(q, k, v, lam_p, subln, lam_init, tq):
    b, t, d = q.shape
    kv_spec = pl.BlockSpec((None, t, ATT_VD), lambda i, h, j: (i, 0, h))
    return pl.pallas_call(
        functools.partial(_flash_body, lam_init=lam_init, tq=tq),
        grid=(b, ATT_HEADS, t // tq),
        in_specs=[
            pl.BlockSpec((None, tq, ATT_VD), lambda i, h, j: (i, j, h)),
            kv_spec,
            kv_spec,
            pl.BlockSpec((4, ATT_DH), lambda i, h, j: (0, 0)),
            pl.BlockSpec((1, ATT_VD), lambda i, h, j: (0, 0)),
        ],
        out_specs=pl.BlockSpec((None, tq, ATT_VD), lambda i, h, j: (i, j, h)),
        out_shape=jax.ShapeDtypeStruct((b, t, d), F32),
        compiler_params=_cparams("parallel", "parallel", "arbitrary"),
        name="flash_prompt",
    )(q, k, v, lam_p, subln.reshape(1, ATT_VD))


def _paged_body(*refs, lam_init, pages_per_step):
    pp = pages_per_step
    pt_ref = refs[0]
    q_ref, kn_ref, vn_ref = refs[1:4]
    kp_refs = refs[4:4 + pp]
    vp_refs = refs[4 + pp:4 + 2 * pp]
    lam_ref, sub_ref, o_ref, qm_ref, m_ref, l_ref, acc_ref = refs[4 + 2 * pp:]
    del pt_ref
    step = pl.program_id(1)
    s_len = o_ref.shape[0]
    grp = q_ref.shape[0]
    nt = (((1,), (1,)), ((), ()))
    heads = range(ATT_HEADS)

    def head_cols(x, h):
        return x[:, h * ATT_VD:(h + 1) * ATT_VD]

    @pl.when(step == 0)
    def _():
        q = q_ref[...] * ATT_SCALE
        r = lax.broadcasted_iota(jnp.int32, q.shape, 0)
        c = lax.broadcasted_iota(jnp.int32, q.shape, 1)
        q = jnp.where((c % ATT_VD) // ATT_DH == r // s_len, q, 0.0).astype(BF16)
        qm_ref[...] = jnp.concatenate([head_cols(q, h) for h in heads], axis=0)
        m_ref[...] = jnp.full(m_ref.shape, NEG_INF, F32)
        l_ref[...] = jnp.zeros(l_ref.shape, F32)
        acc_ref[...] = jnp.zeros(acc_ref.shape, F32)

    qm = qm_ref[...]

    def head_rows(x, h):
        return x[h * grp:(h + 1) * grp]

    def merge(s, weighted_values):
        m = m_ref[...]
        m_new = jnp.maximum(m, jnp.max(s, axis=-1, keepdims=True))
        alpha = jnp.exp(m - m_new)
        p = jnp.exp(s - m_new)
        l_ref[...] = l_ref[...] * alpha + jnp.sum(p, axis=-1, keepdims=True)
        acc_ref[...] = acc_ref[...] * alpha + weighted_values(p.astype(BF16))
        m_ref[...] = m_new

    page_rows = PAGE_SIZE * ATT_HEADS
    own_head = (lax.broadcasted_iota(jnp.int32, (qm.shape[0], page_rows), 0) // grp
                == lax.broadcasted_iota(jnp.int32, (qm.shape[0], page_rows), 1) % ATT_HEADS)
    for kp_ref, vp_ref in zip(kp_refs, vp_refs):
        ka = kp_ref[...].reshape(page_rows, ATT_VD).astype(BF16)
        va = vp_ref[...].reshape(page_rows, ATT_VD).astype(BF16)
        s = lax.dot_general(qm, ka, nt, preferred_element_type=F32)
        merge(jnp.where(own_head, s, NEG_INF), lambda pb: jnp.dot(pb, va, preferred_element_type=F32))

    @pl.when(step == pl.num_programs(1) - 1)
    def _():
        kn = kn_ref[...].astype(BF16)
        vn = vn_ref[...].astype(BF16)
        s = jnp.concatenate([lax.dot_general(head_rows(qm, h), head_cols(kn, h), nt, preferred_element_type=F32)
                             for h in heads], axis=0)
        t = lax.broadcasted_iota(jnp.int32, s.shape, 0) % s_len
        c = lax.broadcasted_iota(jnp.int32, s.shape, 1)
        s = jnp.where(c <= t, s, NEG_INF)
        merge(s, lambda pb: jnp.concatenate(
            [jnp.dot(head_rows(pb, h), head_cols(vn, h), preferred_element_type=F32) for h in heads], axis=0))
        o = acc_ref[...] / l_ref[...]
        lam = _lambda_value(lam_ref, lam_init)
        for h in heads:
            blk = head_rows(o, h)
            oh = blk[:s_len] - lam * blk[s_len:]
            o_ref[:, h * ATT_VD:(h + 1) * ATT_VD] = _rms(oh, sub_ref[...], SUBLN_EPS) * (1.0 - lam_init)


def _paged_sample(q, k_new, v_new, cache_k, cache_v, layer, page_table, lam_p, subln, lam_init, pages_per_step):
    b, s, d = q.shape
    n_pages = page_table.shape[1]
    pp = pages_per_step
    grp = 2 * s
    rows = ATT_HEADS * grp
    q2 = jnp.concatenate([q, q], axis=1)
    k_new, v_new = (jnp.pad(z, ((0, 0), (0, grp - s), (0, 0))) for z in (k_new, v_new))
    tok = pl.BlockSpec((None, grp, d), lambda i, j, pt: (i, 0, 0))

    def page_spec(o):
        return pl.BlockSpec((None, None, PAGE_SIZE, ATT_HEADS, ATT_VD),
                            lambda i, j, pt: (layer, pt[i, j * pp + o], 0, 0, 0))

    grid_spec = pltpu.PrefetchScalarGridSpec(
        num_scalar_prefetch=1,
        grid=(b, n_pages // pp),
        in_specs=[tok, tok, tok] + [page_spec(o) for o in range(pp)] + [page_spec(o) for o in range(pp)] + [
            pl.BlockSpec((4, ATT_DH), lambda i, j, pt: (0, 0)),
            pl.BlockSpec((1, ATT_VD), lambda i, j, pt: (0, 0)),
        ],
        out_specs=pl.BlockSpec((None, s, d), lambda i, j, pt: (i, 0, 0)),
        scratch_shapes=[pltpu.VMEM((rows, ATT_VD), BF16), pltpu.VMEM((rows, 1), F32), pltpu.VMEM((rows, 1), F32),
                        pltpu.VMEM((rows, ATT_VD), F32)],
    )
    return pl.pallas_call(
        functools.partial(_paged_body, lam_init=lam_init, pages_per_step=pp),
        grid_spec=grid_spec,
        out_shape=jax.ShapeDtypeStruct((b, s, d), F32),
        compiler_params=_cparams("parallel", "arbitrary"),
        name="paged_sample",
    )(page_table, q2, k_new, v_new, *([cache_k] * pp), *([cache_v] * pp), lam_p, subln.reshape(1, ATT_VD))


def kernel(x_prompt, x_sample, state_wkv, state_shift, state_conv, cache_k, cache_v, page_table, norm_mix, norm_ffn, norm_final, rwkv_mu, rwkv_w_rkv, rwkv_w0, rwkv_w1, rwkv_w2, rwkv_a0, rwkv_a1, rwkv_a2, rwkv_v0, rwkv_v1, rwkv_v2, rwkv_g1, rwkv_g2, rwkv_k_k, rwkv_k_a, rwkv_r_k, rwkv_ln_g, rwkv_ln_b, rwkv_w_o, conv_w_pw1, conv_b_pw1, conv_w_dw, conv_b_dw, conv_ln_g, conv_ln_b, conv_w_pw2, conv_b_pw2, attn_w_qkv, attn_lambda, attn_subln, attn_w_o, ffn_w13, ffn_w2, moe_router, moe_w13, moe_w2):
    d = D_MODEL
    bf = lambda z: z.astype(BF16)
    w_rkv, w1, w2, a1, a2, v1, v2, g1, g2, w_o = map(
        bf, (rwkv_w_rkv, rwkv_w1, rwkv_w2, rwkv_a1, rwkv_a2, rwkv_v1, rwkv_v2, rwkv_g1, rwkv_g2, rwkv_w_o))
    c_pw1, c_pw2 = bf(conv_w_pw1), bf(conv_w_pw2)
    a_qkv, a_wo = bf(attn_w_qkv), bf(attn_w_o)
    f_w13, f_w2 = bf(ffn_w13), bf(ffn_w2)
    m_w13, m_w2 = bf(moe_w13), bf(moe_w2)
    d_ff = ffn_w2.shape[1]

    def run(x3, wkv_in, shift_in, conv_in, sample):
        b, t, _ = x3.shape
        m = b * t
        x = x3.reshape(m, d)
        wkv_out, shift_out, conv_out, k_out, v_out = [], [], [], [], []
        v_first = None
        for i in range(DEPTH):
            kind, j = i % 3, i // 3
            if kind == 0:
                vres = None if j == 0 else (rwkv_v0[j - 1], v1[j - 1], v2[j - 1])
                x, sh, s_new, v_first = _rwkv_layer(
                    x, b, t, shift_in[j], wkv_in[j], v_first, vres, norm_mix[i], rwkv_mu[j], w_rkv[j],
                    rwkv_w0[j], w1[j], w2[j], rwkv_a0[j], a1[j], a2[j], g1[j], g2[j], rwkv_k_k[j],
                    rwkv_k_a[j], rwkv_r_k[j], rwkv_ln_g[j], rwkv_ln_b[j], w_o[j])
                wkv_out.append(s_new)
                shift_out.append(sh)
            elif kind == 1:
                conv_args = (norm_mix[i], c_pw1[j], conv_b_pw1[j], conv_w_dw[j], conv_b_dw[j], conv_ln_g[j],
                             conv_ln_b[j], c_pw2[j], conv_b_pw2[j])
                if sample:
                    y3, buf = _conv_sample(x.reshape(b, t, d), conv_in[j], *conv_args, 32)
                else:
                    y3, buf = _conv_prompt(x.reshape(b, t, d), conv_in[j], *conv_args, 256)
                x = y3.reshape(m, d)
                conv_out.append(buf)
            else:
                lam_init = 0.8 - 0.6 * math.exp(-0.3 * i)
                q, k, v, q_b, k_b, v_b = _qkv_proj(x, norm_mix[i], a_qkv[j])
                seq = lambda z: z.reshape(b, t, d)
                if sample:
                    o = _paged_sample(seq(q), seq(k), seq(v), cache_k, cache_v, j, page_table, attn_lambda[j],
                                      attn_subln[j], lam_init, 4)
                else:
                    o = _flash_prompt(seq(q_b), seq(k_b), seq(v_b), attn_lambda[j], attn_subln[j], lam_init, 256)
                x = _matmul(o.reshape(m, d), a_wo[j], res=x)
                k_out.append(k.reshape(b, t, ATT_HEADS, 2 * ATT_DH))
                v_out.append(v.reshape(b, t, ATT_HEADS, ATT_VD))
            if i % 2 == 0:
                x = _ffn_dense(x, norm_ffn[i], f_w13[i // 2], f_w2[i // 2], d_ff // 2)
            else:
                x = _moe(x, norm_ffn[i], moe_router[i // 2], m_w13[i // 2], m_w2[i // 2], 512)
        y = _rmsnorm(x, norm_final).reshape(b, t, d)
        return (y, jnp.stack(wkv_out), jnp.stack(shift_out), jnp.stack(conv_out), jnp.stack(k_out),
                jnp.stack(v_out))

    bp = x_prompt.shape[0]
    wkv0 = jnp.zeros((state_wkv.shape[0], bp) + state_wkv.shape[2:], state_wkv.dtype)
    shift0 = jnp.zeros((state_shift.shape[0], bp, d), x_prompt.dtype)
    conv0 = jnp.zeros((state_conv.shape[0], bp, CONV_BUF, d), x_prompt.dtype)
    y_p, wkv_p, sh_p, conv_p, k_p, v_p = run(x_prompt, wkv0, shift0, conv0, False)
    y_s, wkv_s, sh_s, conv_s, k_s, v_s = run(x_sample, state_wkv, state_shift, state_conv, True)
    return (y_p, y_s, wkv_p, wkv_s, sh_p, sh_s, conv_p, conv_s, k_p, k_s, v_p, v_s)
```

```python
import functools
import math

import jax
import jax.numpy as jnp
from jax import lax
from jax.experimental import pallas as pl
from jax.experimental.pallas import tpu as pltpu

F32 = jnp.float32
BF16 = jnp.bfloat16

D_MODEL = 1024
DEPTH = 4
PAGE_SIZE = 128
RWKV_HEAD = 64
RWKV_HEADS = D_MODEL // RWKV_HEAD
GN_EPS = 64e-5
CONV_WIDTH = 31
CONV_BUF = CONV_WIDTH - 1
LN_EPS = 1e-5
ATT_HEADS = 8
ATT_DH = 64
ATT_VD = 128
ATT_SCALE = ATT_DH ** -0.5
SUBLN_EPS = 1e-5
N_EXPERTS = 8
TOP_K = 2
RMS_EPS = 1e-6

V7X_LANES = 128
V7X_SUBLANES = 8
V7X_VMEM_LIMIT_BYTES = 56 * 1024 * 1024

ROW_TILE = 512
EXPERT_TILE_MIN = 128
EXPERT_TILE_MAX = 1024
CONV_HALO = 32
NEG_INF = float("-inf")


def _cparams(*sem):
    return pltpu.CompilerParams(dimension_semantics=sem, vmem_limit_bytes=V7X_VMEM_LIMIT_BYTES)


def _rms(x, g, eps=RMS_EPS):
    return x * lax.rsqrt(jnp.mean(x * x, axis=-1, keepdims=True) + eps) * g


def _sigmoid(x):
    return 1.0 / (1.0 + jnp.exp(-x))


def _silu(x):
    return x * _sigmoid(x)


def _row_tile(m):
    return min(ROW_TILE, m)


def _rms_body(x_ref, g_ref, o_ref):
    o_ref[...] = _rms(x_ref[...], g_ref[...]).astype(o_ref.dtype)


def _rmsnorm(x, g, out_dtype=F32):
    m, d = x.shape
    tm = _row_tile(m)
    return pl.pallas_call(
        _rms_body,
        grid=(m // tm,),
        in_specs=[pl.BlockSpec((tm, d), lambda i: (i, 0)), pl.BlockSpec((1, d), lambda i: (0, 0))],
        out_specs=pl.BlockSpec((tm, d), lambda i: (i, 0)),
        out_shape=jax.ShapeDtypeStruct((m, d), out_dtype),
        compiler_params=_cparams("parallel"),
        name="rmsnorm",
    )(x, g.reshape(1, d))


def _mm_body(*refs, norm, mul, bias, res, glu):
    it = iter(refs)
    x_ref = next(it)
    g_ref = next(it) if norm else None
    m_ref = next(it) if mul else None
    w_ref = next(it)
    w2_ref = next(it) if glu else None
    b_ref = next(it) if bias else None
    b2_ref = next(it) if (bias and glu) else None
    r_ref = next(it) if res else None
    o_ref = next(it)
    xs_ref = next(it)

    @pl.when(pl.program_id(1) == 0)
    def _():
        x = x_ref[...].astype(F32)
        if norm:
            x = _rms(x, g_ref[...])
        if mul:
            x = x * m_ref[...]
        xs_ref[...] = x.astype(BF16)

    xs = xs_ref[...]
    y = jnp.dot(xs, w_ref[...], preferred_element_type=F32)
    if bias:
        y = y + b_ref[...]
    if glu:
        y2 = jnp.dot(xs, w2_ref[...], preferred_element_type=F32)
        if bias:
            y2 = y2 + b2_ref[...]
        y = y * _sigmoid(y2)
    if res:
        y = y + r_ref[...]
    o_ref[...] = y.astype(o_ref.dtype)


def _matmul(x, w, *, norm_g=None, mul=None, bias=None, res=None, glu=False):
    m, k = x.shape
    n = w.shape[1]
    n_out = n // 2 if glu else n
    tm = _row_tile(m)
    tn = n_out
    nj = n_out // tn
    in_specs = [pl.BlockSpec((tm, k), lambda i, j: (i, 0))]
    args = [x]
    if norm_g is not None:
        in_specs.append(pl.BlockSpec((1, k), lambda i, j: (0, 0)))
        args.append(norm_g.reshape(1, k))
    if mul is not None:
        in_specs.append(pl.BlockSpec((tm, k), lambda i, j: (i, 0)))
        args.append(mul)
    in_specs.append(pl.BlockSpec((k, tn), lambda i, j: (0, j)))
    args.append(w)
    if glu:
        in_specs.append(pl.BlockSpec((k, tn), lambda i, j: (0, j + nj)))
        args.append(w)
    if bias is not None:
        b2d = bias.reshape(1, n)
        in_specs.append(pl.BlockSpec((1, tn), lambda i, j: (0, j)))
        args.append(b2d)
        if glu:
            in_specs.append(pl.BlockSpec((1, tn), lambda i, j: (0, j + nj)))
            args.append(b2d)
    if res is not None:
        in_specs.append(pl.BlockSpec((tm, tn), lambda i, j: (i, j)))
        args.append(res)
    body = functools.partial(_mm_body, norm=norm_g is not None, mul=mul is not None,
                             bias=bias is not None, res=res is not None, glu=glu)
    return pl.pallas_call(
        body,
        grid=(m // tm, nj),
        in_specs=in_specs,
        out_specs=pl.BlockSpec((tm, tn), lambda i, j: (i, j)),
        out_shape=jax.ShapeDtypeStruct((m, n_out), F32),
        scratch_shapes=[pltpu.VMEM((tm, k), BF16)],
        compiler_params=_cparams("parallel", "arbitrary"),
        name="matmul",
    )(*args)


def _qkv_body(x_ref, g_ref, w_ref, q_ref, k_ref, v_ref, qb_ref, kb_ref, vb_ref):
    d = q_ref.shape[1]
    xs = _rms(x_ref[...], g_ref[...]).astype(BF16)
    y = jnp.dot(xs, w_ref[...], preferred_element_type=F32)
    for n, (o_ref, ob_ref) in enumerate(((q_ref, qb_ref), (k_ref, kb_ref), (v_ref, vb_ref))):
        part = y[:, n * d:(n + 1) * d]
        o_ref[...] = part
        ob_ref[...] = part.astype(BF16)


def _qkv_proj(x, g, w):
    m, d = x.shape
    tm = min(256, m)
    tok = pl.BlockSpec((tm, d), lambda i: (i, 0))
    return pl.pallas_call(
        _qkv_body,
        grid=(m // tm,),
        in_specs=[tok, pl.BlockSpec((1, d), lambda i: (0, 0)), pl.BlockSpec(w.shape, lambda i: (0, 0))],
        out_specs=[tok] * 6,
        out_shape=[jax.ShapeDtypeStruct((m, d), F32)] * 3 + [jax.ShapeDtypeStruct((m, d), BF16)] * 3,
        compiler_params=_cparams("parallel"),
        name="qkv_proj",
    )(x, g.reshape(1, d), w)


def _tiles_to_rows(val):
    rows, sub, lanes = val.shape
    x = jnp.swapaxes(val.reshape(rows // sub, sub, sub, lanes), 1, 2)
    return jnp.concatenate([x[:, s].reshape(rows, lanes) for s in range(sub)], axis=-1)


def _rows_from_tiles(ref):
    return _tiles_to_rows(ref[...])


def _rows_to_tiles(ref, val):
    rows, sub, lanes = ref.shape
    x = jnp.stack([val[:, s * lanes:(s + 1) * lanes].reshape(rows // sub, sub, lanes) for s in range(sub)],
                  axis=1)
    ref[...] = jnp.swapaxes(x, 1, 2).reshape(rows, sub, lanes)


def _ffn_body(*refs, moe):
    if moe:
        te_ref, tv_ref, x_ref, g_ref, wg_ref, wu_ref, wd_ref, o_ref, hs_ref, acc_ref = refs
    else:
        x_ref, g_ref, wg_ref, wu_ref, wd_ref, o_ref, hs_ref, acc_ref = refs
    j = pl.program_id(1)
    nj = pl.num_programs(1)

    def compute():
        @pl.when(j == 0)
        def _():
            x = _rows_from_tiles(x_ref) if moe else x_ref[...]
            hs_ref[...] = _rms(x, g_ref[...]).astype(BF16)
            acc_ref[...] = jnp.zeros_like(acc_ref)

        h = hs_ref[...]
        gate = jnp.dot(h, wg_ref[...], preferred_element_type=F32)
        up = jnp.dot(h, wu_ref[...], preferred_element_type=F32)
        a = (_silu(gate) * up).astype(BF16)
        acc_ref[...] += jnp.dot(a, wd_ref[...], preferred_element_type=F32)

        @pl.when(j == nj - 1)
        def _():
            if moe:
                _rows_to_tiles(o_ref, acc_ref[...])
            else:
                o_ref[...] = x_ref[...] + acc_ref[...]

    if moe:
        valid = tv_ref[pl.program_id(0)] > 0
        pl.when(valid)(compute)

        @pl.when(jnp.logical_and(jnp.logical_not(valid), j == nj - 1))
        def _():
            o_ref[...] = jnp.zeros_like(o_ref)
    else:
        compute()


def _ffn_dense(x, g, w13, w2, ff_tile):
    m, d = x.shape
    f = w2.shape[0]
    tm = _row_tile(m)
    nf = f // ff_tile
    return pl.pallas_call(
        functools.partial(_ffn_body, moe=False),
        grid=(m // tm, nf),
        in_specs=[
            pl.BlockSpec((tm, d), lambda i, j: (i, 0)),
            pl.BlockSpec((1, d), lambda i, j: (0, 0)),
            pl.BlockSpec((d, ff_tile), lambda i, j: (0, j)),
            pl.BlockSpec((d, ff_tile), lambda i, j: (0, j + nf)),
            pl.BlockSpec((ff_tile, d), lambda i, j: (j, 0)),
        ],
        out_specs=pl.BlockSpec((tm, d), lambda i, j: (i, 0)),
        out_shape=jax.ShapeDtypeStruct((m, d), F32),
        scratch_shapes=[pltpu.VMEM((tm, d), BF16), pltpu.VMEM((tm, d), F32)],
        compiler_params=_cparams("parallel", "arbitrary"),
        name="ffn_dense",
    )(x, g.reshape(1, d), w13, w13, w2)


def _ffn_experts(xs, g, w13, w2, tile_expert, tile_valid, tm, ff_tile):
    s, sub, lanes = xs.shape
    d = sub * lanes
    f = w2.shape[1]
    nf = f // ff_tile
    tok = pl.BlockSpec((tm, sub, lanes), lambda i, j, te, tv: (i, 0, 0))
    grid_spec = pltpu.PrefetchScalarGridSpec(
        num_scalar_prefetch=2,
        grid=(s // tm, nf),
        in_specs=[
            tok,
            pl.BlockSpec((1, d), lambda i, j, te, tv: (0, 0)),
            pl.BlockSpec((None, d, ff_tile), lambda i, j, te, tv: (te[i], 0, j)),
            pl.BlockSpec((None, d, ff_tile), lambda i, j, te, tv: (te[i], 0, j + nf)),
            pl.BlockSpec((None, ff_tile, d), lambda i, j, te, tv: (te[i], j, 0)),
        ],
        out_specs=tok,
        scratch_shapes=[pltpu.VMEM((tm, d), BF16), pltpu.VMEM((tm, d), F32)],
    )
    return pl.pallas_call(
        functools.partial(_ffn_body, moe=True),
        grid_spec=grid_spec,
        out_shape=jax.ShapeDtypeStruct((s, sub, lanes), F32),
        compiler_params=_cparams("parallel", "arbitrary"),
        name="ffn_experts",
    )(tile_expert, tile_valid, xs, g.reshape(1, d), w13, w13, w2)


def _router_body(x_ref, g_ref, whi_ref, wlo_ref, o_ref, g1_ref, g2_ref):
    h = _rms(x_ref[...], g_ref[...])
    hhi = h.astype(BF16)
    hlo = (h - hhi.astype(F32)).astype(BF16)
    whi = whi_ref[...]
    lg = (jnp.dot(hhi, whi, preferred_element_type=F32)
          + jnp.dot(hlo, whi, preferred_element_type=F32)
          + jnp.dot(hhi, wlo_ref[...], preferred_element_type=F32))
    lane = lax.broadcasted_iota(jnp.int32, lg.shape, 1)
    lg = jnp.where(lane < N_EXPERTS, lg, NEG_INF)
    m1 = jnp.max(lg, axis=-1, keepdims=True)
    i1 = jnp.min(jnp.where(lg == m1, lane, V7X_LANES), axis=-1, keepdims=True)
    lg2 = jnp.where(lane == i1, NEG_INF, lg)
    m2 = jnp.max(lg2, axis=-1, keepdims=True)
    i2 = jnp.min(jnp.where(lg2 == m2, lane, V7X_LANES), axis=-1, keepdims=True)
    e2 = jnp.exp(m2 - m1)
    g1 = 1.0 / (1.0 + e2)
    g2 = e2 / (1.0 + e2)
    out = jnp.where(lane == 0, i1.astype(F32),
                    jnp.where(lane == 1, i2.astype(F32),
                              jnp.where(lane == 2, g1, jnp.where(lane == 3, g2, 0.0))))
    o_ref[...] = out
    g1_ref[:, 0, :] = jnp.broadcast_to(g1, lg.shape)
    g2_ref[:, 0, :] = jnp.broadcast_to(g2, lg.shape)


def _router(x, g, w_router):
    m, d = x.shape
    tm = _row_tile(m)
    wpad = jnp.zeros((d, V7X_LANES), F32).at[:, :N_EXPERTS].set(w_router)
    whi = wpad.astype(BF16)
    wlo = (wpad - whi.astype(F32)).astype(BF16)
    gate_spec = pl.BlockSpec((tm, 1, V7X_LANES), lambda i: (i, 0, 0))
    gate_shape = jax.ShapeDtypeStruct((m, 1, V7X_LANES), F32)
    return pl.pallas_call(
        _router_body,
        grid=(m // tm,),
        in_specs=[
            pl.BlockSpec((tm, d), lambda i: (i, 0)),
            pl.BlockSpec((1, d), lambda i: (0, 0)),
            pl.BlockSpec((d, V7X_LANES), lambda i: (0, 0)),
            pl.BlockSpec((d, V7X_LANES), lambda i: (0, 0)),
        ],
        out_specs=[pl.BlockSpec((tm, V7X_LANES), lambda i: (i, 0)), gate_spec, gate_spec],
        out_shape=[jax.ShapeDtypeStruct((m, V7X_LANES), F32), gate_shape, gate_shape],
        compiler_params=_cparams("parallel"),
        name="router",
    )(x, g.reshape(1, d), whi, wlo)


DMA_UNROLL = 8


def _gather_body(src_ref, x_hbm, o_ref, sem):
    tg = o_ref.shape[0]
    base = pl.program_id(0) * tg

    def tile_copy(r):
        return pltpu.make_async_copy(x_hbm.at[pl.ds(src_ref[base + r], 1)], o_ref.at[pl.ds(r, 1)], sem)

    def start(i, c):
        for u in range(DMA_UNROLL):
            tile_copy(i * DMA_UNROLL + u).start(priority=u % 2)
        return c

    def wait(i, c):
        for u in range(DMA_UNROLL):
            tile_copy(i * DMA_UNROLL + u).wait()
        return c

    lax.fori_loop(0, tg // DMA_UNROLL, start, 0)
    lax.fori_loop(0, tg // DMA_UNROLL, wait, 0)


def _gather_tokens(x3, src, tg):
    s = src.shape[0]
    _, sub, lanes = x3.shape
    grid_spec = pltpu.PrefetchScalarGridSpec(
        num_scalar_prefetch=1,
        grid=(s // tg,),
        in_specs=[pl.BlockSpec(memory_space=pl.ANY)],
        out_specs=pl.BlockSpec((tg, sub, lanes), lambda i, src: (i, 0, 0)),
        scratch_shapes=[pltpu.SemaphoreType.DMA(())],
    )
    return pl.pallas_call(
        _gather_body,
        grid_spec=grid_spec,
        out_shape=jax.ShapeDtypeStruct((s, sub, lanes), x3.dtype),
        compiler_params=_cparams("arbitrary"),
        name="moe_gather",
    )(src, x3)


def _combine_body(dest_ref, x_ref, g1_ref, g2_ref, y_hbm, o_ref, buf, sem):
    tc = o_ref.shape[0]
    base = pl.program_id(0) * tc

    def tile_copy(r, k):
        return pltpu.make_async_copy(y_hbm.at[pl.ds(dest_ref[TOP_K * (base + r) + k], 1)],
                                     buf.at[k, pl.ds(r, 1)], sem)

    def start(i, c):
        for u in range(DMA_UNROLL):
            for k in range(TOP_K):
                tile_copy(i * DMA_UNROLL + u, k).start(priority=k)
        return c

    def wait(i, c):
        for u in range(DMA_UNROLL):
            for k in range(TOP_K):
                tile_copy(i * DMA_UNROLL + u, k).wait()
        return c

    lax.fori_loop(0, tc // DMA_UNROLL, start, 0)
    lax.fori_loop(0, tc // DMA_UNROLL, wait, 0)
    o_ref[...] = x_ref[...] + _tiles_to_rows(g1_ref[...] * buf[0] + g2_ref[...] * buf[1])


def _moe_combine(x, g1, g2, y3, dest, tc):
    m, d = x.shape
    _, sub, lanes = y3.shape
    tok = pl.BlockSpec((tc, d), lambda i, dest: (i, 0))
    gate = pl.BlockSpec((tc, 1, lanes), lambda i, dest: (i, 0, 0))
    grid_spec = pltpu.PrefetchScalarGridSpec(
        num_scalar_prefetch=1,
        grid=(m // tc,),
        in_specs=[tok, gate, gate, pl.BlockSpec(memory_space=pl.ANY)],
        out_specs=tok,
        scratch_shapes=[pltpu.VMEM((TOP_K, tc, sub, lanes), F32), pltpu.SemaphoreType.DMA(())],
    )
    return pl.pallas_call(
        _combine_body,
        grid_spec=grid_spec,
        out_shape=jax.ShapeDtypeStruct((m, d), F32),
        compiler_params=_cparams("arbitrary"),
        name="moe_combine",
    )(dest, x, g1, g2, y3)


def _moe(x, g, w_router, w13, w2, ff_tile):
    m, d = x.shape
    tm = max(EXPERT_TILE_MIN, min(EXPERT_TILE_MAX, m // 16))
    r, g1, g2 = _router(x, g, w_router)
    e_flat = r[:, :TOP_K].astype(jnp.int32).reshape(-1)
    n_slots = m * TOP_K
    n_tiles = n_slots // tm + N_EXPERTS
    s_pad = n_tiles * tm
    onehot = (e_flat[:, None] == jnp.arange(N_EXPERTS, dtype=jnp.int32)[None, :]).astype(jnp.int32)
    rank = jnp.sum((jnp.cumsum(onehot, axis=0) - 1) * onehot, axis=1)
    counts = jnp.sum(onehot, axis=0)
    padded = ((counts + tm - 1) // tm) * tm
    ends = jnp.cumsum(padded)
    starts = ends - padded
    dest = (jnp.sum(onehot * starts[None, :], axis=1) + rank).astype(jnp.int32)
    token = jnp.arange(n_slots, dtype=jnp.int32) // TOP_K
    src = (jnp.arange(s_pad, dtype=jnp.int32) % m).at[dest].set(token)
    tile_start = jnp.arange(n_tiles, dtype=jnp.int32) * tm
    tile_valid = (tile_start < ends[-1]).astype(jnp.int32)
    tile_expert = jnp.sum((tile_start[:, None] >= ends[None, :]).astype(jnp.int32), axis=1)
    last_expert = jnp.max(jnp.where(tile_valid > 0, tile_expert, 0))
    tile_expert = jnp.where(tile_valid > 0, tile_expert, last_expert).astype(jnp.int32)
    x3 = x.reshape(m, d // V7X_LANES, V7X_LANES)
    xs = _gather_tokens(x3, src, tm)
    ys = _ffn_experts(xs, g, w13, w2, tile_expert, tile_valid, tm, ff_tile)
    return _moe_combine(x, g1, g2, ys, dest, min(256, m))


def _wkv_body(r_ref, wl_ref, k_ref, v_ref, a_ref, kk_ref, ka_ref, rk_ref, lg_ref, lb_ref, s0_ref,
              y_ref, s_ref, scr):
    tc = pl.program_id(1)
    n = RWKV_HEAD
    half = n // 2
    tt = r_ref.shape[0]
    W1, B1, K1, R1, W2, B2, K2, R2, AN1, AN2 = range(10)

    @pl.when(tc == 0)
    def _():
        s_ref[...] = s0_ref[...]

    kkw = kk_ref[...]
    kaw = ka_ref[...]
    rkw = rk_ref[...]
    lgw = lg_ref[...]
    lbw = lb_ref[...]

    def unit_key(t):
        kk = k_ref[t] * kkw
        return kk / jnp.maximum(jnp.sqrt(jnp.sum(kk * kk, axis=0, keepdims=True)), 1e-12)

    def decay(t):
        return jnp.exp(-jnp.exp(wl_ref[t]))

    def row(slot, i, rows):
        return jnp.broadcast_to(scr[slot, pl.ds(i, 1), :], (rows, V7X_LANES))

    def set_next(t1, t2):
        scr[AN1] = -unit_key(t1)
        scr[AN2] = decay(t1) * -unit_key(t2)

    def emit(t, y, r, k2, v):
        ym = jnp.mean(y, axis=0, keepdims=True)
        yc = y - ym
        yn = yc * lax.rsqrt(jnp.mean(yc * yc, axis=0, keepdims=True) + GN_EPS)
        bonus = jnp.sum(r * k2 * rkw, axis=0, keepdims=True) * v
        y_ref[t] = yn * lgw + lbw + bonus

    set_next(0, 1)
    sa0 = jnp.zeros((n, V7X_LANES), F32)
    sb0 = jnp.zeros((n, V7X_LANES), F32)
    for i in range(n):
        s_i = s_ref[i]
        sa0 = sa0 + s_i * row(AN1, i, n)
        sb0 = sb0 + s_i * row(AN2, i, n)

    def pair(p, carry):
        sa1, sb = carry
        t1 = 2 * p
        t2 = t1 + 1
        r1, k1, v1, a1 = r_ref[t1], k_ref[t1], v_ref[t1], a_ref[t1]
        r2, k2, v2, a2 = r_ref[t2], k_ref[t2], v_ref[t2], a_ref[t2]
        kk1 = unit_key(t1)
        kk2 = unit_key(t2)
        k1m = k1 * (1.0 + (a1 - 1.0) * kaw)
        k2m = k2 * (1.0 + (a2 - 1.0) * kaw)
        b1 = kk1 * a1
        sa2 = (sb + sa1 * jnp.sum(b1 * -kk2, axis=0, keepdims=True)
               + v1 * jnp.sum(k1m * -kk2, axis=0, keepdims=True))
        scr[W1] = decay(t1)
        scr[B1] = b1
        scr[K1] = k1m
        scr[R1] = r1
        scr[W2] = decay(t2)
        scr[B2] = kk2 * a2
        scr[K2] = k2m
        scr[R2] = r2
        set_next(jnp.minimum(t1 + 2, tt - 1), jnp.minimum(t1 + 3, tt - 1))
        ys1, ys2, nas, nbs = [], [], [], []
        for vh in range(2):
            rows = slice(vh * half, (vh + 1) * half)
            sa1_h, sa2_h, v1_h, v2_h = sa1[rows], sa2[rows], v1[rows], v2[rows]

            def key_group(gi, acc):
                y1, y2, na, nb = acc
                base = pl.multiple_of(gi * V7X_SUBLANES, V7X_SUBLANES)
                for u in range(V7X_SUBLANES):
                    i = base + u
                    s1 = s_ref[i, rows, :] * row(W1, i, half) + sa1_h * row(B1, i, half) + v1_h * row(K1, i, half)
                    s2 = s1 * row(W2, i, half) + sa2_h * row(B2, i, half) + v2_h * row(K2, i, half)
                    s_ref[i, rows, :] = s2
                    y1 = y1 + s1 * row(R1, i, half)
                    y2 = y2 + s2 * row(R2, i, half)
                    na = na + s2 * row(AN1, i, half)
                    nb = nb + s2 * row(AN2, i, half)
                return y1, y2, na, nb

            zero = jnp.zeros((half, V7X_LANES), F32)
            y1, y2, na, nb = lax.fori_loop(0, n // V7X_SUBLANES, key_group, (zero, zero, zero, zero))
            ys1.append(y1)
            ys2.append(y2)
            nas.append(na)
            nbs.append(nb)
        emit(t1, jnp.concatenate(ys1, axis=0), r1, k1m, v1)
        emit(t2, jnp.concatenate(ys2, axis=0), r2, k2m, v2)
        return jnp.concatenate(nas, axis=0), jnp.concatenate(nbs, axis=0)

    lax.fori_loop(0, tt // 2, pair, (sa0, sb0))


def _wkv(r, wl, k, v, a, k_k, k_a, r_k, ln_g, ln_b, s0, time_tile):
    t, n, l = r.shape
    tt = min(time_tile, t)
    seq = pl.BlockSpec((tt, n, V7X_LANES), lambda i, j: (j, 0, i))
    par = pl.BlockSpec((n, V7X_LANES), lambda i, j: (0, i))
    st = pl.BlockSpec((n, n, V7X_LANES), lambda i, j: (0, 0, i))
    return pl.pallas_call(
        _wkv_body,
        grid=(l // V7X_LANES, t // tt),
        in_specs=[seq, seq, seq, seq, seq, par, par, par, par, par, st],
        out_specs=[seq, st],
        out_shape=[jax.ShapeDtypeStruct((t, n, l), F32), jax.ShapeDtypeStruct((n, n, l), F32)],
        scratch_shapes=[pltpu.VMEM((10, n, V7X_LANES), F32)],
        compiler_params=_cparams("parallel", "arbitrary"),
        name="wkv7",
    )(r, wl, k, v, a, k_k, k_a, r_k, ln_g, ln_b, s0)


def _softplus(z):
    return jnp.maximum(z, 0.0) + jnp.log1p(jnp.exp(-jnp.abs(z)))


def _rwkv_proj_body(*refs, has_vres, shift_in_kernel):
    (h_ref, hp_ref, mu_ref, wr_ref, wk_ref, wv_ref, w0_ref, w1_ref, w2_ref,
     a0_ref, a1_ref, a2_ref, g1_ref, g2_ref) = refs[:14]
    n_in = 14
    if has_vres:
        vf_ref, v0_ref, v1_ref, v2_ref = refs[14:18]
        n_in = 18
    r_ref, k_ref, v_ref, wl_ref, a_ref, g_ref = refs[n_in:n_in + 6]
    h = h_ref[...]
    if shift_in_kernel:
        ext_ref = refs[n_in + 6]
        tm = h.shape[0]
        ext_ref[pl.ds(V7X_SUBLANES - 1, 1), :] = hp_ref[...]
        ext_ref[pl.ds(V7X_SUBLANES, tm), :] = h
        h_prev = ext_ref[pl.ds(V7X_SUBLANES - 1, tm), :]
    else:
        h_prev = hp_ref[...]
    xx = h_prev - h

    def mix(i):
        return (h + xx * mu_ref[pl.ds(i, 1), :]).astype(BF16)

    def dot(x, w_ref):
        return jnp.dot(x.astype(BF16), w_ref[...], preferred_element_type=F32)

    r_ref[...] = dot(mix(0), wr_ref)
    z = w0_ref[...] + dot(jnp.tanh(dot(mix(1), w1_ref)), w2_ref)
    wl_ref[...] = -_softplus(-z) - 0.5
    k_ref[...] = dot(mix(2), wk_ref)
    xv = mix(3)
    v = dot(xv, wv_ref)
    if has_vres:
        v = v + (vf_ref[...] - v) * _sigmoid(v0_ref[...] + dot(dot(xv, v1_ref), v2_ref))
    v_ref[...] = v
    a_ref[...] = _sigmoid(a0_ref[...] + dot(dot(mix(4), a1_ref), a2_ref))
    g_ref[...] = dot(_sigmoid(dot(mix(5), g1_ref)), g2_ref)


RWKV_PROJ_TILE = 256


def _rwkv_proj(h, h_prev, mu, w_rkv, w0, w1, w2, a0, a1, a2, g1, g2, v_first, vres):
    m, c = h.shape
    tm = min(RWKV_PROJ_TILE, m)
    tok = pl.BlockSpec((tm, c), lambda i: (i, 0))
    shift_in_kernel = h_prev.ndim == 3
    prev_spec = pl.BlockSpec((None, 1, c), lambda i: (i, 0, 0)) if shift_in_kernel else tok

    def whole(z):
        return pl.BlockSpec(z.shape, lambda i: (0,) * z.ndim)

    row = lambda z: z.reshape(1, c)
    consts = [mu, w_rkv[0], w_rkv[1], w_rkv[2], row(w0), w1, w2, row(a0), a1, a2, g1, g2]
    args = [h, h_prev] + consts
    in_specs = [tok, prev_spec] + [whole(z) for z in consts]
    if vres is not None:
        v0, v1, v2 = vres
        extra = [row(v0), v1, v2]
        args += [v_first] + extra
        in_specs += [tok] + [whole(z) for z in extra]
    return pl.pallas_call(
        functools.partial(_rwkv_proj_body, has_vres=vres is not None, shift_in_kernel=shift_in_kernel),
        grid=(m // tm,),
        in_specs=in_specs,
        out_specs=[tok] * 6,
        out_shape=[jax.ShapeDtypeStruct((m, c), F32)] * 6,
        scratch_shapes=[pltpu.VMEM((V7X_SUBLANES + tm, c), F32)] if shift_in_kernel else [],
        compiler_params=_cparams("parallel"),
        name="rwkv_proj",
    )(*args)


def _to_lanes(z, b, t):
    return jnp.transpose(z.reshape(b, t, RWKV_HEADS, RWKV_HEAD), (1, 3, 0, 2)).reshape(t, RWKV_HEAD, b * RWKV_HEADS)


def _head_param(p, b):
    ph = p.reshape(RWKV_HEADS, RWKV_HEAD).T
    return jnp.tile(ph, (1, b))


def _rwkv_layer(x, b, t, shift_prev, s0, v_first, vres, norm_g, mu, w_rkv, w0, w1, w2, a0, a1, a2,
                g1, g2, k_k, k_a, r_k, ln_g, ln_b, w_o):
    m, c = x.shape
    h = _rmsnorm(x, norm_g)
    h3 = h.reshape(b, t, c)
    tile = RWKV_PROJ_TILE
    if t % tile == 0:
        h_prev = jnp.concatenate([shift_prev[:, None, :], h3[:, tile - 1:t - 1:tile]], axis=1)
        h_prev = h_prev.reshape(m // tile, 1, c)
    else:
        h_prev = jnp.concatenate([shift_prev[:, None, :], h3[:, :-1]], axis=1).reshape(m, c)
    r, k, v, wl, a, g = _rwkv_proj(h, h_prev, mu, w_rkv, w0, w1, w2, a0, a1, a2, g1, g2, v_first, vres)
    if vres is None:
        v_first = v
    tl = lambda z: _to_lanes(z, b, t)
    hp = lambda p: _head_param(p, b)
    s0_l = jnp.transpose(s0, (3, 2, 0, 1)).reshape(RWKV_HEAD, RWKV_HEAD, b * RWKV_HEADS)
    y_l, s_l = _wkv(tl(r), tl(wl), tl(k), tl(v), tl(a), hp(k_k), hp(k_a), hp(r_k), hp(ln_g), hp(ln_b),
                    s0_l, 64)
    y = jnp.transpose(y_l.reshape(t, RWKV_HEAD, b, RWKV_HEADS), (2, 0, 3, 1)).reshape(m, c)
    s_new = jnp.transpose(s_l.reshape(RWKV_HEAD, RWKV_HEAD, b, RWKV_HEADS), (2, 3, 1, 0))
    out = _matmul(y, w_o, mul=g, res=x)
    return out, h3[:, -1], s_new, v_first


def _conv_tail(c, ln_g, ln_b):
    mu = jnp.mean(c, axis=-1, keepdims=True)
    xc = c - mu
    var = jnp.mean(xc * xc, axis=-1, keepdims=True)
    return _silu(xc * lax.rsqrt(var + LN_EPS) * ln_g + ln_b)


def _conv_prompt_body(x_ref, buf_ref, g_ref, wa_ref, wb_ref, ba_ref, bb_ref, wdw_ref, bdw_ref,
                      lg_ref, lb_ref, w2_ref, b2_ref, o_ref, tail_ref, ext_ref):
    ti = pl.program_id(1)
    tm = x_ref.shape[0]

    @pl.when(ti == 0)
    def _():
        ext_ref[pl.ds(0, CONV_HALO), :] = buf_ref[...]

    x = x_ref[...]
    h = _rms(x, g_ref[...]).astype(BF16)
    ua = jnp.dot(h, wa_ref[...], preferred_element_type=F32) + ba_ref[...]
    ub = jnp.dot(h, wb_ref[...], preferred_element_type=F32) + bb_ref[...]
    ext_ref[pl.ds(CONV_HALO, tm), :] = ua * _sigmoid(ub)
    c = jnp.zeros_like(x) + bdw_ref[...]
    for j in range(CONV_WIDTH):
        c = c + ext_ref[pl.ds(CONV_HALO - CONV_BUF + j, tm), :] * wdw_ref[pl.ds(j, 1), :]
    z = _conv_tail(c, lg_ref[...], lb_ref[...]).astype(BF16)
    o_ref[...] = x + jnp.dot(z, w2_ref[...], preferred_element_type=F32) + b2_ref[...]
    tail = ext_ref[pl.ds(tm, CONV_HALO), :]
    ext_ref[pl.ds(0, CONV_HALO), :] = tail

    @pl.when(ti == pl.num_programs(1) - 1)
    def _():
        tail_ref[...] = tail


def _conv_prompt(x, buf, norm_g, w_pw1, b_pw1, w_dw, b_dw, ln_g, ln_b, w_pw2, b_pw2, tm):
    b, t, d = x.shape
    bufp = jnp.pad(buf, ((0, 0), (CONV_HALO - CONV_BUF, 0), (0, 0)))
    row = lambda z: z.reshape(1, -1)
    vec = pl.BlockSpec((1, d), lambda i, j: (0, 0))
    out, tail = pl.pallas_call(
        _conv_prompt_body,
        grid=(b, t // tm),
        in_specs=[
            pl.BlockSpec((None, tm, d), lambda i, j: (i, j, 0)),
            pl.BlockSpec((None, CONV_HALO, d), lambda i, j: (i, 0, 0)),
            vec,
            pl.BlockSpec((d, d), lambda i, j: (0, 0)),
            pl.BlockSpec((d, d), lambda i, j: (0, 1)),
            pl.BlockSpec((1, d), lambda i, j: (0, 0)),
            pl.BlockSpec((1, d), lambda i, j: (0, 1)),
            pl.BlockSpec((CONV_WIDTH, d), lambda i, j: (0, 0)),
            vec, vec, vec,
            pl.BlockSpec((d, d), lambda i, j: (0, 0)),
            vec,
        ],
        out_specs=[pl.BlockSpec((None, tm, d), lambda i, j: (i, j, 0)),
                   pl.BlockSpec((None, CONV_HALO, d), lambda i, j: (i, 0, 0))],
        out_shape=[jax.ShapeDtypeStruct((b, t, d), F32), jax.ShapeDtypeStruct((b, CONV_HALO, d), F32)],
        scratch_shapes=[pltpu.VMEM((CONV_HALO + tm, d), F32)],
        compiler_params=_cparams("parallel", "arbitrary"),
        name="conv_prompt",
    )(x, bufp, row(norm_g), w_pw1, w_pw1, row(b_pw1), row(b_pw1), w_dw, row(b_dw), row(ln_g), row(ln_b),
      w_pw2, row(b_pw2))
    return out, tail[:, CONV_HALO - CONV_BUF:, :]


def _conv_sample_body(x_ref, ext_ref, wdw_ref, bdw_ref, lg_ref, lb_ref, w2_ref, b2_ref, o_ref):
    s = x_ref.shape[0]
    for t in range(s):
        c = jnp.zeros(x_ref.shape[1:], F32) + bdw_ref[...]
        for j in range(CONV_WIDTH):
            c = c + ext_ref[t + j] * wdw_ref[pl.ds(j, 1), :]
        z = _conv_tail(c, lg_ref[...], lb_ref[...]).astype(BF16)
        o_ref[t] = x_ref[t] + jnp.dot(z, w2_ref[...], preferred_element_type=F32) + b2_ref[...]


def _conv_sample(x, buf, norm_g, w_pw1, b_pw1, w_dw, b_dw, ln_g, ln_b, w_pw2, b_pw2, bb):
    b, s, d = x.shape
    u = _matmul(x.reshape(b * s, d), w_pw1, norm_g=norm_g, bias=b_pw1, glu=True).reshape(b, s, d)
    ext = jnp.concatenate([buf, u], axis=1)
    ext_t = jnp.transpose(ext, (1, 0, 2))
    x_t = jnp.transpose(x, (1, 0, 2))
    row = lambda z: z.reshape(1, -1)
    vec = pl.BlockSpec((1, d), lambda i: (0, 0))
    out_t = pl.pallas_call(
        _conv_sample_body,
        grid=(b // bb,),
        in_specs=[
            pl.BlockSpec((s, bb, d), lambda i: (0, i, 0)),
            pl.BlockSpec((CONV_BUF + s, bb, d), lambda i: (0, i, 0)),
            pl.BlockSpec((CONV_WIDTH, d), lambda i: (0, 0)),
            vec, vec, vec,
            pl.BlockSpec((d, d), lambda i: (0, 0)),
            vec,
        ],
        out_specs=pl.BlockSpec((s, bb, d), lambda i: (0, i, 0)),
        out_shape=jax.ShapeDtypeStruct((s, b, d), F32),
        compiler_params=_cparams("parallel"),
        name="conv_sample",
    )(x_t, ext_t, w_dw, row(b_dw), row(ln_g), row(ln_b), w_pw2, row(b_pw2))
    return jnp.transpose(out_t, (1, 0, 2)), ext[:, -CONV_BUF:, :]


def _lambda_value(lam_ref, lam_init):
    lp = lam_ref[...]
    l1 = jnp.sum(lp[0:1, :] * lp[1:2, :], axis=-1, keepdims=True)
    l2 = jnp.sum(lp[2:3, :] * lp[3:4, :], axis=-1, keepdims=True)
    return jnp.exp(l1) - jnp.exp(l2) + lam_init


def _flash_body(q_ref, k_ref, v_ref, lam_ref, sub_ref, o_ref, *, lam_init, tq):
    qi = pl.program_id(2)
    q = q_ref[...] * ATT_SCALE
    lane = lax.broadcasted_iota(jnp.int32, q.shape, 1)
    zero = jnp.zeros_like(q)
    qs = jnp.concatenate([jnp.where(lane < ATT_DH, q, zero), jnp.where(lane >= ATT_DH, q, zero)], axis=0)
    nt = (((1,), (1,)), ((), ()))

    def chunk(start, width, carry, masked):
        m, l, acc = carry
        kc = k_ref[pl.ds(pl.multiple_of(start, tq), width), :]
        vc = v_ref[pl.ds(pl.multiple_of(start, tq), width), :]
        s = lax.dot_general(qs, kc, nt, preferred_element_type=F32)
        if masked:
            row = lax.broadcasted_iota(jnp.int32, s.shape, 0) % tq
            col = lax.broadcasted_iota(jnp.int32, s.shape, 1)
            s = jnp.where(col <= row, s, NEG_INF)
        m_new = jnp.maximum(m, jnp.max(s, axis=-1, keepdims=True))
        alpha = jnp.exp(m - m_new)
        p = jnp.exp(s - m_new)
        l = l * alpha + jnp.sum(p, axis=-1, keepdims=True)
        acc = acc * alpha + jnp.dot(p.astype(BF16), vc, preferred_element_type=F32)
        return m_new, l, acc

    init = (jnp.full((2 * tq, 1), NEG_INF, F32), jnp.zeros((2 * tq, 1), F32), jnp.zeros((2 * tq, ATT_VD), F32))
    carry = lax.fori_loop(0, qi // 2, lambda c, cr: chunk(c * 2 * tq, 2 * tq, cr, False), init)
    carry = lax.cond(qi % 2 == 1, lambda cr: chunk((qi - 1) * tq, tq, cr, False), lambda cr: cr, carry)
    _, l, acc = chunk(qi * tq, tq, carry, True)
    lam = _lambda_value(lam_ref, lam_init)
    o = acc / l
    o = o[:tq] - lam * o[tq:]
    o_ref[...] = _rms(o, sub_ref[...], SUBLN_EPS) * (1.0 - lam_init)


def _flash_prompt(q, k, v, lam_p, subln, lam_init, tq):
    b, t, d = q.shape
    kv_spec = pl.BlockSpec((None, t, ATT_VD), lambda i, h, j: (i, 0, h))
    return pl.pallas_call(
        functools.partial(_flash_body, lam_init=lam_init, tq=tq),
        grid=(b, ATT_HEADS, t // tq),
        in_specs=[
            pl.BlockSpec((None, tq, ATT_VD), lambda i, h, j: (i, j, h)),
            kv_spec,
            kv_spec,
            pl.BlockSpec((4, ATT_DH), lambda i, h, j: (0, 0)),
            pl.BlockSpec((1, ATT_VD), lambda i, h, j: (0, 0)),
        ],
        out_specs=pl.BlockSpec((None, tq, ATT_VD), lambda i, h, j: (i, j, h)),
        out_shape=jax.ShapeDtypeStruct((b, t, d), F32),
        compiler_params=_cparams("parallel", "parallel", "arbitrary"),
        name="flash_prompt",
    )(q, k, v, lam_p, subln.reshape(1, ATT_VD))


def _paged_body(*refs, lam_init, pages_per_step):
    pp = pages_per_step
    pt_ref = refs[0]
    q_ref, kn_ref, vn_ref = refs[1:4]
    kp_refs = refs[4:4 + pp]
    vp_refs = refs[4 + pp:4 + 2 * pp]
    lam_ref, sub_ref, o_ref, qm_ref, m_ref, l_ref, acc_ref = refs[4 + 2 * pp:]
    del pt_ref
    step = pl.program_id(1)
    s_len = o_ref.shape[0]
    grp = q_ref.shape[0]
    nt = (((1,), (1,)), ((), ()))
    heads = range(ATT_HEADS)

    def head_cols(x, h):
        return x[:, h * ATT_VD:(h + 1) * ATT_VD]

    @pl.when(step == 0)
    def _():
        q = q_ref[...] * ATT_SCALE
        r = lax.broadcasted_iota(jnp.int32, q.shape, 0)
        c = lax.broadcasted_iota(jnp.int32, q.shape, 1)
        q = jnp.where((c % ATT_VD) // ATT_DH == r // s_len, q, 0.0).astype(BF16)
        qm_ref[...] = jnp.concatenate([head_cols(q, h) for h in heads], axis=0)
        m_ref[...] = jnp.full(m_ref.shape, NEG_INF, F32)
        l_ref[...] = jnp.zeros(l_ref.shape, F32)
        acc_ref[...] = jnp.zeros(acc_ref.shape, F32)

    qm = qm_ref[...]

    def head_rows(x, h):
        return x[h * grp:(h + 1) * grp]

    def merge(s, weighted_values):
        m = m_ref[...]
        m_new = jnp.maximum(m, jnp.max(s, axis=-1, keepdims=True))
        alpha = jnp.exp(m - m_new)
        p = jnp.exp(s - m_new)
        l_ref[...] = l_ref[...] * alpha + jnp.sum(p, axis=-1, keepdims=True)
        acc_ref[...] = acc_ref[...] * alpha + weighted_values(p.astype(BF16))
        m_ref[...] = m_new

    page_rows = PAGE_SIZE * ATT_HEADS
    own_head = (lax.broadcasted_iota(jnp.int32, (qm.shape[0], page_rows), 0) // grp
                == lax.broadcasted_iota(jnp.int32, (qm.shape[0], page_rows), 1) % ATT_HEADS)
    for kp_ref, vp_ref in zip(kp_refs, vp_refs):
        ka = kp_ref[...].reshape(page_rows, ATT_VD).astype(BF16)
        va = vp_ref[...].reshape(page_rows, ATT_VD).astype(BF16)
        s = lax.dot_general(qm, ka, nt, preferred_element_type=F32)
        merge(jnp.where(own_head, s, NEG_INF), lambda pb: jnp.dot(pb, va, preferred_element_type=F32))

    @pl.when(step == pl.num_programs(1) - 1)
    def _():
        kn = kn_ref[...].astype(BF16)
        vn = vn_ref[...].astype(BF16)
        s = jnp.concatenate([lax.dot_general(head_rows(qm, h), head_cols(kn, h), nt, preferred_element_type=F32)
                             for h in heads], axis=0)
        t = lax.broadcasted_iota(jnp.int32, s.shape, 0) % s_len
        c = lax.broadcasted_iota(jnp.int32, s.shape, 1)
        s = jnp.where(c <= t, s, NEG_INF)
        merge(s, lambda pb: jnp.concatenate(
            [jnp.dot(head_rows(pb, h), head_cols(vn, h), preferred_element_type=F32) for h in heads], axis=0))
        o = acc_ref[...] / l_ref[...]
        lam = _lambda_value(lam_ref, lam_init)
        for h in heads:
            blk = head_rows(o, h)
            oh = blk[:s_len] - lam * blk[s_len:]
            o_ref[:, h * ATT_VD:(h + 1) * ATT_VD] = _rms(oh, sub_ref[...], SUBLN_EPS) * (1.0 - lam_init)


def _paged_sample(q, k_new, v_new, cache_k, cache_v, layer, page_table, lam_p, subln, lam_init, pages_per_step):
    b, s, d = q.shape
    n_pages = page_table.shape[1]
    pp = pages_per_step
    grp = 2 * s
    rows = ATT_HEADS * grp
    q2 = jnp.concatenate([q, q], axis=1)
    k_new, v_new = (jnp.pad(z, ((0, 0), (0, grp - s), (0, 0))) for z in (k_new, v_new))
    tok = pl.BlockSpec((None, grp, d), lambda i, j, pt: (i, 0, 0))

    def page_spec(o):
        return pl.BlockSpec((None, None, PAGE_SIZE, ATT_HEADS, ATT_VD),
                            lambda i, j, pt: (layer, pt[i, j * pp + o], 0, 0, 0))

    grid_spec = pltpu.PrefetchScalarGridSpec(
        num_scalar_prefetch=1,
        grid=(b, n_pages // pp),
        in_specs=[tok, tok, tok] + [page_spec(o) for o in range(pp)] + [page_spec(o) for o in range(pp)] + [
            pl.BlockSpec((4, ATT_DH), lambda i, j, pt: (0, 0)),
            pl.BlockSpec((1, ATT_VD), lambda i, j, pt: (0, 0)),
        ],
        out_specs=pl.BlockSpec((None, s, d), lambda i, j, pt: (i, 0, 0)),
        scratch_shapes=[pltpu.VMEM((rows, ATT_VD), BF16), pltpu.VMEM((rows, 1), F32), pltpu.VMEM((rows, 1), F32),
                        pltpu.VMEM((rows, ATT_VD), F32)],
    )
    return pl.pallas_call(
        functools.partial(_paged_body, lam_init=lam_init, pages_per_step=pp),
        grid_spec=grid_spec,
        out_shape=jax.ShapeDtypeStruct((b, s, d), F32),
        compiler_params=_cparams("parallel", "arbitrary"),
        name="paged_sample",
    )(page_table, q2, k_new, v_new, *([cache_k] * pp), *([cache_v] * pp), lam_p, subln.reshape(1, ATT_VD))


def kernel(x_prompt, x_sample, state_wkv, state_shift, state_conv, cache_k, cache_v, page_table, norm_mix, norm_ffn, norm_final, rwkv_mu, rwkv_w_rkv, rwkv_w0, rwkv_w1, rwkv_w2, rwkv_a0, rwkv_a1, rwkv_a2, rwkv_v0, rwkv_v1, rwkv_v2, rwkv_g1, rwkv_g2, rwkv_k_k, rwkv_k_a, rwkv_r_k, rwkv_ln_g, rwkv_ln_b, rwkv_w_o, conv_w_pw1, conv_b_pw1, conv_w_dw, conv_b_dw, conv_ln_g, conv_ln_b, conv_w_pw2, conv_b_pw2, attn_w_qkv, attn_lambda, attn_subln, attn_w_o, ffn_w13, ffn_w2, moe_router, moe_w13, moe_w2):
    d = D_MODEL
    bf = lambda z: z.astype(BF16)
    w_rkv, w1, w2, a1, a2, v1, v2, g1, g2, w_o = map(
        bf, (rwkv_w_rkv, rwkv_w1, rwkv_w2, rwkv_a1, rwkv_a2, rwkv_v1, rwkv_v2, rwkv_g1, rwkv_g2, rwkv_w_o))
    c_pw1, c_pw2 = bf(conv_w_pw1), bf(conv_w_pw2)
    a_qkv, a_wo = bf(attn_w_qkv), bf(attn_w_o)
    f_w13, f_w2 = bf(ffn_w13), bf(ffn_w2)
    m_w13, m_w2 = bf(moe_w13), bf(moe_w2)
    d_ff = ffn_w2.shape[1]

    def run(x3, wkv_in, shift_in, conv_in, sample):
        b, t, _ = x3.shape
        m = b * t
        x = x3.reshape(m, d)
        wkv_out, shift_out, conv_out, k_out, v_out = [], [], [], [], []
        v_first = None
        for i in range(DEPTH):
            kind, j = i % 3, i // 3
            if kind == 0:
                vres = None if j == 0 else (rwkv_v0[j - 1], v1[j - 1], v2[j - 1])
                x, sh, s_new, v_first = _rwkv_layer(
                    x, b, t, shift_in[j], wkv_in[j], v_first, vres, norm_mix[i], rwkv_mu[j], w_rkv[j],
                    rwkv_w0[j], w1[j], w2[j], rwkv_a0[j], a1[j], a2[j], g1[j], g2[j], rwkv_k_k[j],
                    rwkv_k_a[j], rwkv_r_k[j], rwkv_ln_g[j], rwkv_ln_b[j], w_o[j])
                wkv_out.append(s_new)
                shift_out.append(sh)
            elif kind == 1:
                conv_args = (norm_mix[i], c_pw1[j], conv_b_pw1[j], conv_w_dw[j], conv_b_dw[j], conv_ln_g[j],
                             conv_ln_b[j], c_pw2[j], conv_b_pw2[j])
                if sample:
                    y3, buf = _conv_sample(x.reshape(b, t, d), conv_in[j], *conv_args, 32)
                else:
                    y3, buf = _conv_prompt(x.reshape(b, t, d), conv_in[j], *conv_args, 256)
                x = y3.reshape(m, d)
                conv_out.append(buf)
            else:
                lam_init = 0.8 - 0.6 * math.exp(-0.3 * i)
                q, k, v, q_b, k_b, v_b = _qkv_proj(x, norm_mix[i], a_qkv[j])
                seq = lambda z: z.reshape(b, t, d)
                if sample:
                    o = _paged_sample(seq(q), seq(k), seq(v), cache_k, cache_v, j, page_table, attn_lambda[j],
                                      attn_subln[j], lam_init, 8)
                else:
                    o = _flash_prompt(seq(q_b), seq(k_b), seq(v_b), attn_lambda[j], attn_subln[j], lam_init, 256)
                x = _matmul(o.reshape(m, d), a_wo[j], res=x)
                k_out.append(k.reshape(b, t, ATT_HEADS, 2 * ATT_DH))
                v_out.append(v.reshape(b, t, ATT_HEADS, ATT_VD))
            if i % 2 == 0:
                x = _ffn_dense(x, norm_ffn[i], f_w13[i // 2], f_w2[i // 2], d_ff // 2)
            else:
                x = _moe(x, norm_ffn[i], moe_router[i // 2], m_w13[i // 2], m_w2[i // 2], 1792 if sample else 512)
        y = _rmsnorm(x, norm_final).reshape(b, t, d)
        return (y, jnp.stack(wkv_out), jnp.stack(shift_out), jnp.stack(conv_out), jnp.stack(k_out),
                jnp.stack(v_out))

    bp = x_prompt.shape[0]
    wkv0 = jnp.zeros((state_wkv.shape[0], bp) + state_wkv.shape[2:], state_wkv.dtype)
    shift0 = jnp.zeros((state_shift.shape[0], bp, d), x_prompt.dtype)
    conv0 = jnp.zeros((state_conv.shape[0], bp, CONV_BUF, d), x_prompt.dtype)
    y_p, wkv_p, sh_p, conv_p, k_p, v_p = run(x_prompt, wkv0, shift0, conv0, False)
    y_s, wkv_s, sh_s, conv_s, k_s, v_s = run(x_sample, state_wkv, state_shift, state_conv, True)
    return (y_p, y_s, wkv_p, wkv_s, sh_p, sh_s, conv_p, conv_s, k_p, k_s, v_p, v_s)
```
